```python
import math
import jax, jax.numpy as jnp
from jax import lax
import numpy as np

D_MODEL = 1024
BATCH = 8
SEQ = 2048
DEPTH = 4

N_A = DEPTH // 2
N_B = DEPTH - N_A
D_FF = 4 * D_MODEL
CONV_WIDTH = 31
HEAD_DIM = 64
N_HEADS = D_MODEL // (2 * HEAD_DIM)
QK_DIM = 2 * N_HEADS * HEAD_DIM
V_DIM = N_HEADS * 2 * HEAD_DIM
NUM_BUCKETS = 32
MAX_DISTANCE = 128
PLE_DIM = 256
Q_BLOCK = 128
RMS_EPS = 1e-6
LN_EPS = 1e-5
NEG_INF = -1e30

kernel_name = "yoco_conformer_diffattn_macaron"


def rms_norm(x, g, eps=RMS_EPS):
    xf = x.astype(jnp.float32)
    y = xf * lax.rsqrt(jnp.mean(xf * xf, axis=-1, keepdims=True) + eps)
    return (y * g.astype(jnp.float32)).astype(x.dtype)


def layer_norm(x, g, b, eps=LN_EPS):
    xf = x.astype(jnp.float32)
    mu = jnp.mean(xf, axis=-1, keepdims=True)
    var = jnp.mean(jnp.square(xf - mu), axis=-1, keepdims=True)
    y = (xf - mu) * lax.rsqrt(var + eps) * g.astype(jnp.float32) + b.astype(jnp.float32)
    return y.astype(x.dtype)


def swiglu_ffn(h, w_in, w_out):
    g, u = jnp.split(h @ w_in, 2, axis=-1)
    return (jax.nn.silu(g) * u) @ w_out


def conformer_conv(h, w_in, b_in, w_dw, b_dw, ln_g, ln_b, w_out, b_out):
    a, gate = jnp.split(h @ w_in + b_in, 2, axis=-1)
    u = a * jax.nn.sigmoid(gate)
    u = lax.conv_general_dilated(
        u, w_dw[:, None, :].astype(u.dtype), window_strides=(1,),
        padding=((CONV_WIDTH - 1, 0),),
        dimension_numbers=("NWC", "WIO", "NWC"),
        feature_group_count=D_MODEL) + b_dw
    u = jax.nn.silu(layer_norm(u, ln_g, ln_b))
    return u @ w_out + b_out


def rel_bucket(q_pos, k_pos):
    n = jnp.maximum(q_pos[:, None] - k_pos[None, :], 0)
    max_exact = NUM_BUCKETS // 2
    nf = jnp.maximum(n, 1).astype(jnp.float32)
    large = max_exact + (jnp.log(nf / max_exact) / math.log(MAX_DISTANCE / max_exact)
                         * (NUM_BUCKETS - max_exact)).astype(jnp.int32)
    large = jnp.minimum(large, NUM_BUCKETS - 1)
    return jnp.where(n < max_exact, n, large)


def diff_attention(h, k, v, w_q, lq1, lk1, lq2, lk2, subln, w_o, rel_bias, lambda_init):
    B, S, _ = h.shape
    nb = S // Q_BLOCK
    scale = HEAD_DIM ** -0.5
    q = (h @ w_q).reshape(B, nb, Q_BLOCK, 2, N_HEADS, HEAD_DIM).transpose(1, 0, 2, 3, 4, 5)
    k1, k2 = k[:, :, 0], k[:, :, 1]
    f32 = jnp.float32
    lam = (jnp.exp(jnp.sum(lq1.astype(f32) * lk1.astype(f32)))
           - jnp.exp(jnp.sum(lq2.astype(f32) * lk2.astype(f32))) + lambda_init)
    k_pos = jnp.arange(S)

    def block(args):
        i, qblk = args
        q_pos = i * Q_BLOCK + jnp.arange(Q_BLOCK)
        bias = rel_bias[rel_bucket(q_pos, k_pos)].astype(f32).transpose(2, 0, 1)
        mask = k_pos[None, :] <= q_pos[:, None]

        def probs(qh, kh):
            s = jnp.einsum("bqhd,bkhd->bhqk", qh, kh).astype(f32) * scale + bias
            return jax.nn.softmax(jnp.where(mask, s, NEG_INF), axis=-1)

        a = probs(qblk[:, :, 0], k1) - lam * probs(qblk[:, :, 1], k2)
        return jnp.einsum("bhqk,bkhe->bqhe", a, v)

    o = lax.map(block, (jnp.arange(nb), q))
    o = o.transpose(1, 0, 2, 3, 4).reshape(B, S, N_HEADS, 2 * HEAD_DIM)
    o = rms_norm(o, subln) * (1.0 - lambda_init)
    return o.reshape(B, S, V_DIM).astype(h.dtype) @ w_o


def setup_inputs(seed: int = 0) -> dict:
    key = jax.random.key(seed)
    ks = iter(jax.random.split(key, 48))
    f32 = jnp.float32

    def nrm(shape, fan_in):
        return jax.random.normal(next(ks), shape, f32) * (fan_in ** -0.5)

    def gain(shape):
        return 1.0 + 0.02 * jax.random.normal(next(ks), shape, f32)

    def small(shape, s=0.02):
        return s * jax.random.normal(next(ks), shape, f32)

    return {
        "x": jax.random.normal(next(ks), (BATCH, SEQ, D_MODEL), f32),
        "p": jax.random.normal(next(ks), (DEPTH, BATCH, SEQ, PLE_DIM), f32),
        "ffn1_norm": gain((DEPTH, D_MODEL)),
        "ffn1_w_in": nrm((DEPTH, D_MODEL, 2 * D_FF), D_MODEL),
        "ffn1_w_out": nrm((DEPTH, D_FF, D_MODEL), D_FF),
        "mix_norm": gain((DEPTH, D_MODEL)),
        "ffn2_norm": gain((DEPTH, D_MODEL)),
        "ffn2_w_in": nrm((DEPTH, D_MODEL, 2 * D_FF), D_MODEL),
        "ffn2_w_out": nrm((DEPTH, D_FF, D_MODEL), D_FF),
        "ple_norm": gain((DEPTH, D_MODEL)),
        "ple_w_gate": nrm((DEPTH, D_MODEL, D_MODEL), D_MODEL),
        "ple_w_proj": nrm((DEPTH, PLE_DIM, D_MODEL), PLE_DIM),
        "conv_w_in": nrm((N_A, D_MODEL, 2 * D_MODEL), D_MODEL),
        "conv_b_in": small((N_A, 2 * D_MODEL)),
        "conv_w_dw": nrm((N_A, CONV_WIDTH, D_MODEL), CONV_WIDTH),
        "conv_b_dw": small((N_A, D_MODEL)),
        "conv_ln_g": gain((N_A, D_MODEL)),
        "conv_ln_b": small((N_A, D_MODEL)),
        "conv_w_out": nrm((N_A, D_MODEL, D_MODEL), D_MODEL),
        "conv_b_out": small((N_A, D_MODEL)),
        "kv_norm": gain((D_MODEL,)),
        "w_kv": nrm((D_MODEL, QK_DIM + V_DIM), D_MODEL),
        "attn_w_q": nrm((N_B, D_MODEL, QK_DIM), D_MODEL),
        "attn_lq1": small((N_B, HEAD_DIM), 0.1),
        "attn_lk1": small((N_B, HEAD_DIM), 0.1),
        "attn_lq2": small((N_B, HEAD_DIM), 0.1),
        "attn_lk2": small((N_B, HEAD_DIM), 0.1),
        "attn_subln": gain((N_B, 2 * HEAD_DIM)),
        "attn_w_o": nrm((N_B, V_DIM, D_MODEL), V_DIM),
        "rel_bias": small((NUM_BUCKETS, N_HEADS), 0.5),
        "final_norm": gain((D_MODEL,)),
    }


def reference(x, p, ffn1_norm, ffn1_w_in, ffn1_w_out, mix_norm, ffn2_norm, ffn2_w_in, ffn2_w_out,
              ple_norm, ple_w_gate, ple_w_proj, conv_w_in, conv_b_in, conv_w_dw, conv_b_dw,
              conv_ln_g, conv_ln_b, conv_w_out, conv_b_out, kv_norm, w_kv, attn_w_q,
              attn_lq1, attn_lk1, attn_lq2, attn_lk2, attn_subln, attn_w_o, rel_bias, final_norm):
    B, S, _ = x.shape
    h = x
    k_shared = None
    v_shared = None
    for i in range(DEPTH):
        if i == N_A:
            kv = rms_norm(h, kv_norm) @ w_kv
            k_shared = kv[..., :QK_DIM].reshape(B, S, 2, N_HEADS, HEAD_DIM)
            v_shared = kv[..., QK_DIM:].reshape(B, S, N_HEADS, 2 * HEAD_DIM)
        h = h + 0.5 * swiglu_ffn(rms_norm(h, ffn1_norm[i]), ffn1_w_in[i], ffn1_w_out[i])
        hn = rms_norm(h, mix_norm[i])
        if i < N_A:
            h = h + conformer_conv(hn, conv_w_in[i], conv_b_in[i], conv_w_dw[i], conv_b_dw[i],
                                   conv_ln_g[i], conv_ln_b[i], conv_w_out[i], conv_b_out[i])
        else:
            j = i - N_A
            lambda_init = 0.8 - 0.6 * math.exp(-0.3 * i)
            h = h + diff_attention(hn, k_shared, v_shared, attn_w_q[j], attn_lq1[j], attn_lk1[j],
                                   attn_lq2[j], attn_lk2[j], attn_subln[j], attn_w_o[j],
                                   rel_bias, lambda_init)
        h = h + 0.5 * swiglu_ffn(rms_norm(h, ffn2_norm[i]), ffn2_w_in[i], ffn2_w_out[i])
        gate = jax.nn.sigmoid(rms_norm(h, ple_norm[i]) @ ple_w_gate[i])
        h = h + gate * (p[i] @ ple_w_proj[i])
    return rms_norm(h, final_norm)
```

```python
import functools
import math

import jax
import jax.numpy as jnp
from jax import lax
from jax.experimental import pallas as pl
from jax.experimental.pallas import tpu as pltpu

D_MODEL = 1024
DEPTH = 4
N_A = DEPTH // 2
D_FF = 4 * D_MODEL
CONV_WIDTH = 31
HEAD_DIM = 64
N_HEADS = D_MODEL // (2 * HEAD_DIM)
QK_DIM = 2 * N_HEADS * HEAD_DIM
V_DIM = N_HEADS * 2 * HEAD_DIM
NUM_BUCKETS = 32
MAX_DISTANCE = 128
PLE_DIM = 256
RMS_EPS = 1e-6
LN_EPS = 1e-5
NEG_INF = -1e30

LANES = 128
SUBLANES = 8
VMEM_LIMIT_BYTES = 56 * 1024 * 1024

FFN_TM = 1024
FFN_TF = 512
PROJ_TM = 1024
CONV_TS = 512
CONV_HALO = 32
CONV_ROWS = 32
ATT_T = 256

F32 = jnp.float32
BF16 = jnp.bfloat16


def _rms(x, g):
    ms = jnp.mean(x * x, axis=-1, keepdims=True)
    return x * lax.rsqrt(ms + RMS_EPS) * g


def _sigmoid(x):
    return 1.0 / (1.0 + jnp.exp(-x))


def _params(sem):
    return pltpu.CompilerParams(dimension_semantics=sem, vmem_limit_bytes=VMEM_LIMIT_BYTES)


def _ffn_body(*refs, n_f, ple, final):
    h_ref, g_ref, wg_ref, wu_ref, wo_ref = refs[:5]
    pos = 5
    if ple:
        pn_ref, pwg_ref, p_ref, pwp_ref = refs[pos:pos + 4]
        pos += 4
    if final:
        fn_ref = refs[pos]
        pos += 1
    out_ref, xn_ref, acc_ref = refs[pos:pos + 3]

    f = pl.program_id(1)

    @pl.when(f == 0)
    def _():
        xn_ref[...] = _rms(h_ref[...], g_ref[...]).astype(BF16)

    xn = xn_ref[...]
    gate = jnp.dot(xn, wg_ref[...], preferred_element_type=F32)
    up = jnp.dot(xn, wu_ref[...], preferred_element_type=F32)
    act = (gate * _sigmoid(gate) * up).astype(BF16)
    part = jnp.dot(act, wo_ref[...], preferred_element_type=F32)

    @pl.when(f == 0)
    def _():
        acc_ref[...] = part

    @pl.when(f > 0)
    def _():
        acc_ref[...] += part

    @pl.when(f == n_f - 1)
    def _():
        hn = h_ref[...] + 0.5 * acc_ref[...]
        if ple:
            xg = _rms(hn, pn_ref[...]).astype(BF16)
            gt = _sigmoid(jnp.dot(xg, pwg_ref[...], preferred_element_type=F32))
            pr = jnp.dot(p_ref[...].astype(BF16), pwp_ref[...], preferred_element_type=F32)
            hn = hn + gt * pr
        if final:
            hn = _rms(hn, fn_ref[...])
        out_ref[...] = hn


def _ffn(h, norm_g, w_in, w_out, ple=None, final_g=None):
    t = h.shape[0]
    tm, tf = FFN_TM, FFN_TF
    n_f = D_FF // tf
    row = lambda i, f: (i, 0)
    fixed = lambda i, f: (0, 0)
    in_specs = [
        pl.BlockSpec((tm, D_MODEL), row),
        pl.BlockSpec((1, D_MODEL), fixed),
        pl.BlockSpec((D_MODEL, tf), lambda i, f: (0, f)),
        pl.BlockSpec((D_MODEL, tf), lambda i, f: (0, f + n_f)),
        pl.BlockSpec((tf, D_MODEL), lambda i, f: (f, 0)),
    ]
    args = [h, norm_g.reshape(1, D_MODEL), w_in, w_in, w_out]
    if ple is not None:
        pn, pwg, p, pwp = ple
        in_specs += [
            pl.BlockSpec((1, D_MODEL), fixed),
            pl.BlockSpec((D_MODEL, D_MODEL), fixed),
            pl.BlockSpec((tm, PLE_DIM), row),
            pl.BlockSpec((PLE_DIM, D_MODEL), fixed),
        ]
        args += [pn.reshape(1, D_MODEL), pwg, p, pwp]
    if final_g is not None:
        in_specs.append(pl.BlockSpec((1, D_MODEL), fixed))
        args.append(final_g.reshape(1, D_MODEL))
    body = functools.partial(_ffn_body, n_f=n_f, ple=ple is not None, final=final_g is not None)
    return pl.pallas_call(
        body,
        grid=(t // tm, n_f),
        in_specs=in_specs,
        out_specs=pl.BlockSpec((tm, D_MODEL), row),
        out_shape=jax.ShapeDtypeStruct((t, D_MODEL), F32),
        scratch_shapes=[pltpu.VMEM((tm, D_MODEL), BF16), pltpu.VMEM((tm, D_MODEL), F32)],
        compiler_params=_params(("parallel", "arbitrary")),
        name="ffn",
    )(*args)


def _norm_proj_body(h_ref, g_ref, w_ref, *out_refs):
    xn = _rms(h_ref[...], g_ref[...]).astype(BF16)
    y = jnp.dot(xn, w_ref[...], preferred_element_type=F32)
    for n, o_ref in enumerate(out_refs):
        o_ref[...] = y[:, n * D_MODEL:(n + 1) * D_MODEL].astype(o_ref.dtype)


def _norm_proj(h, norm_g, w, n_out):
    t = h.shape[0]
    tm = PROJ_TM
    row = lambda i: (i, 0)
    fixed = lambda i: (0, 0)
    return pl.pallas_call(
        _norm_proj_body,
        grid=(t // tm,),
        in_specs=[
            pl.BlockSpec((tm, D_MODEL), row),
            pl.BlockSpec((1, D_MODEL), fixed),
            pl.BlockSpec((D_MODEL, n_out * D_MODEL), fixed),
        ],
        out_specs=[pl.BlockSpec((tm, D_MODEL), row)] * n_out,
        out_shape=[jax.ShapeDtypeStruct((t, D_MODEL), BF16)] * n_out,
        compiler_params=_params(("parallel",)),
        name="norm_proj",
    )(h, norm_g.reshape(1, D_MODEL), w)


def _out_proj_body(h_ref, o_ref, w_ref, out_ref):
    out_ref[...] = h_ref[...] + jnp.dot(o_ref[...], w_ref[...], preferred_element_type=F32)


def _out_proj(h, o, w):
    t = h.shape[0]
    tm = PROJ_TM
    row = lambda i: (i, 0)
    return pl.pallas_call(
        _out_proj_body,
        grid=(t // tm,),
        in_specs=[
            pl.BlockSpec((tm, D_MODEL), row),
            pl.BlockSpec((tm, V_DIM), row),
            pl.BlockSpec((V_DIM, D_MODEL), lambda i: (0, 0)),
        ],
        out_specs=pl.BlockSpec((tm, D_MODEL), row),
        out_shape=jax.ShapeDtypeStruct((t, D_MODEL), F32),
        compiler_params=_params(("parallel",)),
        name="out_proj",
    )(h, o, w)


def _conv_body(h_ref, g_ref, win_ref, bin_ref, wdw_ref, bdw_ref, lng_ref, lnb_ref,
               wout_ref, bout_ref, out_ref, ubuf_ref, cbuf_ref, *, ts):
    n_lb = D_MODEL // LANES
    halo = CONV_HALO
    rows = CONV_ROWS
    first_tap_row = halo - (CONV_WIDTH - 1)

    @pl.when(pl.program_id(1) == 0)
    def _():
        ubuf_ref[:, 0:halo, :] = jnp.zeros((n_lb, halo, LANES), F32)

    x = h_ref[...]
    xn = _rms(x, g_ref[...]).astype(BF16)
    y = jnp.dot(xn, win_ref[...], preferred_element_type=F32) + bin_ref[...]
    u = y[:, :D_MODEL] * _sigmoid(y[:, D_MODEL:])
    for j in range(n_lb):
        ubuf_ref[j, halo:halo + ts, :] = u[:, j * LANES:(j + 1) * LANES]

    def lane_block(j, carry):
        for r in range(ts // rows):
            acc = jnp.zeros((rows // SUBLANES, SUBLANES, LANES), F32)
            for k in range(CONV_WIDTH):
                seg = ubuf_ref[j, pl.ds(r * rows + first_tap_row + k, rows), :]
                acc = acc + seg.reshape(rows // SUBLANES, SUBLANES, LANES) * wdw_ref[j, k][None]
            cbuf_ref[j, r * rows:(r + 1) * rows, :] = acc.reshape(rows, LANES)
        return carry

    lax.fori_loop(0, n_lb, lane_block, 0)

    ubuf_ref[:, 0:halo, :] = ubuf_ref[:, ts:ts + halo, :]

    c = jnp.concatenate([cbuf_ref[j] for j in range(n_lb)], axis=1) + bdw_ref[...]
    mu = jnp.mean(c, axis=-1, keepdims=True)
    d = c - mu
    var = jnp.mean(d * d, axis=-1, keepdims=True)
    z = d * lax.rsqrt(var + LN_EPS) * lng_ref[...] + lnb_ref[...]
    z = (z * _sigmoid(z)).astype(BF16)
    out_ref[...] = x + jnp.dot(z, wout_ref[...], preferred_element_type=F32) + bout_ref[...]


def _conv_layer(h, batch, seq, norm_g, w_in, b_in, w_dw, b_dw, ln_g, ln_b, w_out, b_out):
    ts = CONV_TS
    n_s = seq // ts
    n_lb = D_MODEL // LANES
    wdw = jnp.broadcast_to(
        w_dw.reshape(CONV_WIDTH, n_lb, 1, LANES).transpose(1, 0, 2, 3),
        (n_lb, CONV_WIDTH, SUBLANES, LANES))
    row = lambda b, s: (b * n_s + s, 0)
    fixed = lambda b, s: (0, 0)
    vec = pl.BlockSpec((1, D_MODEL), fixed)
    return pl.pallas_call(
        functools.partial(_conv_body, ts=ts),
        grid=(batch, n_s),
        in_specs=[
            pl.BlockSpec((ts, D_MODEL), row),
            vec,
            pl.BlockSpec((D_MODEL, 2 * D_MODEL), fixed),
            pl.BlockSpec((1, 2 * D_MODEL), fixed),
            pl.BlockSpec((n_lb, CONV_WIDTH, SUBLANES, LANES), lambda b, s: (0, 0, 0, 0)),
            vec, vec, vec,
            pl.BlockSpec((D_MODEL, D_MODEL), fixed),
            vec,
        ],
        out_specs=pl.BlockSpec((ts, D_MODEL), row),
        out_shape=jax.ShapeDtypeStruct(h.shape, F32),
        scratch_shapes=[
            pltpu.VMEM((n_lb, CONV_HALO + ts, LANES), F32),
            pltpu.VMEM((n_lb, ts, LANES), F32),
        ],
        compiler_params=_params(("parallel", "arbitrary")),
        name="conformer_conv",
    )(h, norm_g.reshape(1, -1), w_in, b_in.reshape(1, -1), wdw, b_dw.reshape(1, -1),
      ln_g.reshape(1, -1), ln_b.reshape(1, -1), w_out, b_out.reshape(1, -1))


def _attn_body(q_ref, k_ref, v_ref, bias_ref, lam_ref, sub_ref, out_ref,
               m1_ref, m2_ref, acc1_ref, acc2_ref, *, t, lambda_init):
    i = pl.program_id(2)
    q = q_ref[...]
    lane = lax.broadcasted_iota(jnp.int32, q.shape, 1)
    zero = jnp.zeros_like(q)
    q1 = jnp.where(lane < HEAD_DIM, q, zero)
    q2 = jnp.where(lane >= HEAD_DIM, q, zero)
    ones = jnp.ones((t, LANES), BF16)

    m1_ref[...] = jnp.full(m1_ref.shape, NEG_INF, F32)
    m2_ref[...] = jnp.full(m2_ref.shape, NEG_INF, F32)
    acc1_ref[...] = jnp.zeros(acc1_ref.shape, F32)
    acc2_ref[...] = jnp.zeros(acc2_ref.shape, F32)

    def chunk(j, bias):
        start = pl.multiple_of(j * t, t)
        kc = k_ref[pl.ds(start, t), :]
        vc = v_ref[pl.ds(start, t), :]
        v_aug = jnp.concatenate([vc, ones], axis=1)
        for qm, m_ref, acc_ref in ((q1, m1_ref, acc1_ref), (q2, m2_ref, acc2_ref)):
            s = lax.dot_general(qm, kc, (((1,), (1,)), ((), ())), preferred_element_type=F32)
            if bias is not None:
                s = s + bias
            m_old = m_ref[...]
            m_new = jnp.maximum(m_old, jnp.max(s, axis=-1, keepdims=True))
            alpha = jnp.exp(m_old - m_new)
            p = jnp.exp(s - m_new).astype(BF16)
            acc_ref[...] = alpha * acc_ref[...] + jnp.dot(p, v_aug, preferred_element_type=F32)
            m_ref[...] = m_new

    def far(j, carry):
        chunk(j, None)
        return carry

    lax.fori_loop(0, jnp.maximum(i - 1, 0), far, 0)

    @pl.when(i >= 1)
    def _():
        chunk(i - 1, bias_ref[1])

    chunk(i, bias_ref[0])

    lq = lam_ref[...]
    lam = (jnp.exp(jnp.sum(lq[0:1] * lq[1:2], axis=-1, keepdims=True))
           - jnp.exp(jnp.sum(lq[2:3] * lq[3:4], axis=-1, keepdims=True)) + lambda_init)
    a1 = acc1_ref[...]
    a2 = acc2_ref[...]
    o = (a1[:, :LANES] / a1[:, LANES:LANES + 1]
         - lam * (a2[:, :LANES] / a2[:, LANES:LANES + 1]))
    out_ref[...] = (_rms(o, sub_ref[...]) * (1.0 - lambda_init)).astype(out_ref.dtype)


def _rel_bias_tiles(rel_bias, t):
    assert t >= MAX_DISTANCE
    qp = jnp.arange(t)[:, None]
    kp = jnp.arange(t)[None, :]
    max_exact = NUM_BUCKETS // 2

    def bucket(n):
        nf = jnp.maximum(n, 1).astype(F32)
        large = max_exact + (jnp.log(nf / max_exact) / math.log(MAX_DISTANCE / max_exact)
                             * (NUM_BUCKETS - max_exact)).astype(jnp.int32)
        large = jnp.minimum(large, NUM_BUCKETS - 1)
        return jnp.where(n < max_exact, n, large)

    rb = rel_bias.astype(F32)
    far_const = rb[NUM_BUCKETS - 1]
    n0 = jnp.maximum(qp - kp, 0)
    diag = rb[bucket(n0)].transpose(2, 0, 1) - far_const[:, None, None]
    diag = jnp.where((kp <= qp)[None], diag, NEG_INF)
    left = rb[bucket(qp + t - kp)].transpose(2, 0, 1) - far_const[:, None, None]
    return jnp.stack([diag, left], axis=1)


def _attention(q, k, v, bias_tiles, lam_rows, subln, batch, seq, lambda_init):
    t = ATT_T
    n_q = seq // t
    q3 = q.reshape(batch, seq, QK_DIM)
    k3 = k.reshape(batch, seq, QK_DIM)
    v3 = v.reshape(batch, seq, V_DIM)
    out = pl.pallas_call(
        functools.partial(_attn_body, t=t, lambda_init=lambda_init),
        grid=(batch, N_HEADS, n_q),
        in_specs=[
            pl.BlockSpec((None, t, LANES), lambda b, h, i: (b, i, h)),
            pl.BlockSpec((None, seq, LANES), lambda b, h, i: (b, 0, h)),
            pl.BlockSpec((None, seq, LANES), lambda b, h, i: (b, 0, h)),
            pl.BlockSpec((None, 2, t, t), lambda b, h, i: (h, 0, 0, 0)),
            pl.BlockSpec((4, HEAD_DIM), lambda b, h, i: (0, 0)),
            pl.BlockSpec((1, LANES), lambda b, h, i: (0, 0)),
        ],
        out_specs=pl.BlockSpec((None, t, LANES), lambda b, h, i: (b, i, h)),
        out_shape=jax.ShapeDtypeStruct((batch, seq, V_DIM), BF16),
        scratch_shapes=[
            pltpu.VMEM((t, 1), F32), pltpu.VMEM((t, 1), F32),
            pltpu.VMEM((t, 2 * LANES), F32), pltpu.VMEM((t, 2 * LANES), F32),
        ],
        compiler_params=_params(("parallel", "parallel", "arbitrary")),
        name="diff_attention",
    )(q3, k3, v3, bias_tiles, lam_rows, subln.reshape(1, LANES))
    return out.reshape(batch * seq, V_DIM)


def _head_major(w):
    d = w.shape[0]
    return w.reshape(d, 2, N_HEADS, HEAD_DIM).transpose(0, 2, 1, 3).reshape(d, QK_DIM)


def kernel(x, p, ffn1_norm, ffn1_w_in, ffn1_w_out, mix_norm, ffn2_norm, ffn2_w_in, ffn2_w_out,
           ple_norm, ple_w_gate, ple_w_proj, conv_w_in, conv_b_in, conv_w_dw, conv_b_dw,
           conv_ln_g, conv_ln_b, conv_w_out, conv_b_out, kv_norm, w_kv, attn_w_q,
           attn_lq1, attn_lk1, attn_lq2, attn_lk2, attn_subln, attn_w_o, rel_bias, final_norm):
    batch, seq, _ = x.shape
    tokens = batch * seq
    h = x.reshape(tokens, D_MODEL)
    p2 = p.reshape(DEPTH, tokens, PLE_DIM)
    bias_tiles = _rel_bias_tiles(rel_bias, ATT_T)
    w_kv_hm = jnp.concatenate([_head_major(w_kv[:, :QK_DIM]), w_kv[:, QK_DIM:]], axis=1).astype(BF16)
    k = v = None
    for i in range(DEPTH):
        if i == N_A:
            k, v = _norm_proj(h, kv_norm, w_kv_hm, 2)
        h = _ffn(h, ffn1_norm[i], ffn1_w_in[i].astype(BF16), ffn1_w_out[i].astype(BF16))
        if i < N_A:
            h = _conv_layer(h, batch, seq, mix_norm[i], conv_w_in[i].astype(BF16), conv_b_in[i],
                            conv_w_dw[i], conv_b_dw[i], conv_ln_g[i], conv_ln_b[i],
                            conv_w_out[i].astype(BF16), conv_b_out[i])
        else:
            j = i - N_A
            lambda_init = 0.8 - 0.6 * math.exp(-0.3 * i)
            w_q = (_head_major(attn_w_q[j]) * (HEAD_DIM ** -0.5)).astype(BF16)
            (q,) = _norm_proj(h, mix_norm[i], w_q, 1)
            lam_rows = jnp.stack([attn_lq1[j], attn_lk1[j], attn_lq2[j], attn_lk2[j]]).astype(F32)
            o = _attention(q, k, v, bias_tiles, lam_rows, attn_subln[j], batch, seq, lambda_init)
            h = _out_proj(h, o, attn_w_o[j].astype(BF16))
        h = _ffn(h, ffn2_norm[i], ffn2_w_in[i].astype(BF16), ffn2_w_out[i].astype(BF16),
                 ple=(ple_norm[i], ple_w_gate[i].astype(BF16), p2[i], ple_w_proj[i].astype(BF16)),
                 final_g=final_norm if i == DEPTH - 1 else None)
    return h.reshape(batch, seq, D_MODEL)
```

```python
import functools
import math

import jax
import jax.numpy as jnp
import numpy as np
from jax import lax
from jax.experimental import pallas as pl
from jax.experimental.pallas import tpu as pltpu

D_MODEL = 1024
DEPTH = 4
N_A = DEPTH // 2
D_FF = 4 * D_MODEL
CONV_WIDTH = 31
HEAD_DIM = 64
N_HEADS = D_MODEL // (2 * HEAD_DIM)
QK_DIM = 2 * N_HEADS * HEAD_DIM
V_DIM = N_HEADS * 2 * HEAD_DIM
NUM_BUCKETS = 32
MAX_DISTANCE = 128
PLE_DIM = 256
RMS_EPS = 1e-6
LN_EPS = 1e-5
NEG_INF = -1e30

LANES = 128
SUBLANES = 8
VMEM_LIMIT_BYTES = 56 * 1024 * 1024

FFN_TM = 1024
FFN_TF = 512
PROJ_TM = 1024
CONV_TS = 512
CONV_HALO = 32
CONV_ROWS = 32
ATT_T = 256

F32 = jnp.float32
BF16 = jnp.bfloat16


def _rms(x, g):
    ms = jnp.mean(x * x, axis=-1, keepdims=True)
    return x * lax.rsqrt(ms + RMS_EPS) * g


def _sigmoid(x):
    return 1.0 / (1.0 + jnp.exp(-x))


def _params(sem):
    return pltpu.CompilerParams(dimension_semantics=sem, vmem_limit_bytes=VMEM_LIMIT_BYTES)


def _ffn_body(*refs, n_f, ple, final):
    h_ref, g_ref, wg_ref, wu_ref, wo_ref = refs[:5]
    pos = 5
    if ple:
        pn_ref, pwg_ref, p_ref, pwp_ref = refs[pos:pos + 4]
        pos += 4
    if final:
        fn_ref = refs[pos]
        pos += 1
    out_ref, xn_ref, acc_ref = refs[pos:pos + 3]

    f = pl.program_id(1)

    @pl.when(f == 0)
    def _():
        xn_ref[...] = _rms(h_ref[...], g_ref[...]).astype(BF16)

    xn = xn_ref[...]
    gate = jnp.dot(xn, wg_ref[...], preferred_element_type=F32)
    up = jnp.dot(xn, wu_ref[...], preferred_element_type=F32)
    act = (gate * _sigmoid(gate) * up).astype(BF16)
    part = jnp.dot(act, wo_ref[...], preferred_element_type=F32)

    @pl.when(f == 0)
    def _():
        acc_ref[...] = part

    @pl.when(f > 0)
    def _():
        acc_ref[...] += part

    @pl.when(f == n_f - 1)
    def _():
        hn = h_ref[...] + 0.5 * acc_ref[...]
        if ple:
            xg = _rms(hn, pn_ref[...]).astype(BF16)
            gt = _sigmoid(jnp.dot(xg, pwg_ref[...], preferred_element_type=F32))
            pr = jnp.dot(p_ref[...].astype(BF16), pwp_ref[...], preferred_element_type=F32)
            hn = hn + gt * pr
        if final:
            hn = _rms(hn, fn_ref[...])
        out_ref[...] = hn


def _ffn(h, norm_g, w_in, w_out, ple=None, final_g=None):
    t = h.shape[0]
    tm, tf = FFN_TM, FFN_TF
    n_f = D_FF // tf
    row = lambda i, f: (i, 0)
    fixed = lambda i, f: (0, 0)
    in_specs = [
        pl.BlockSpec((tm, D_MODEL), row),
        pl.BlockSpec((1, D_MODEL), fixed),
        pl.BlockSpec((D_MODEL, tf), lambda i, f: (0, f)),
        pl.BlockSpec((D_MODEL, tf), lambda i, f: (0, f + n_f)),
        pl.BlockSpec((tf, D_MODEL), lambda i, f: (f, 0)),
    ]
    args = [h, norm_g.reshape(1, D_MODEL), w_in, w_in, w_out]
    if ple is not None:
        pn, pwg, p, pwp = ple
        in_specs += [
            pl.BlockSpec((1, D_MODEL), fixed),
            pl.BlockSpec((D_MODEL, D_MODEL), fixed),
            pl.BlockSpec((tm, PLE_DIM), row),
            pl.BlockSpec((PLE_DIM, D_MODEL), fixed),
        ]
        args += [pn.reshape(1, D_MODEL), pwg, p, pwp]
    if final_g is not None:
        in_specs.append(pl.BlockSpec((1, D_MODEL), fixed))
        args.append(final_g.reshape(1, D_MODEL))
    body = functools.partial(_ffn_body, n_f=n_f, ple=ple is not None, final=final_g is not None)
    return pl.pallas_call(
        body,
        grid=(t // tm, n_f),
        in_specs=in_specs,
        out_specs=pl.BlockSpec((tm, D_MODEL), row),
        out_shape=jax.ShapeDtypeStruct((t, D_MODEL), F32),
        scratch_shapes=[pltpu.VMEM((tm, D_MODEL), BF16), pltpu.VMEM((tm, D_MODEL), F32)],
        compiler_params=_params(("parallel", "arbitrary")),
        name="ffn",
    )(*args)


def _norm_proj_body(h_ref, g_ref, w_ref, *out_refs):
    xn = _rms(h_ref[...], g_ref[...]).astype(BF16)
    y = jnp.dot(xn, w_ref[...], preferred_element_type=F32)
    for n, o_ref in enumerate(out_refs):
        o_ref[...] = y[:, n * D_MODEL:(n + 1) * D_MODEL].astype(o_ref.dtype)


def _norm_proj(h, norm_g, w, n_out):
    t = h.shape[0]
    tm = PROJ_TM
    row = lambda i: (i, 0)
    fixed = lambda i: (0, 0)
    return pl.pallas_call(
        _norm_proj_body,
        grid=(t // tm,),
        in_specs=[
            pl.BlockSpec((tm, D_MODEL), row),
            pl.BlockSpec((1, D_MODEL), fixed),
            pl.BlockSpec((D_MODEL, n_out * D_MODEL), fixed),
        ],
        out_specs=[pl.BlockSpec((tm, D_MODEL), row)] * n_out,
        out_shape=[jax.ShapeDtypeStruct((t, D_MODEL), BF16)] * n_out,
        compiler_params=_params(("parallel",)),
        name="norm_proj",
    )(h, norm_g.reshape(1, D_MODEL), w)


def _out_proj_body(h_ref, o_ref, w_ref, out_ref):
    out_ref[...] = h_ref[...] + jnp.dot(o_ref[...], w_ref[...], preferred_element_type=F32)


def _out_proj(h, o, w):
    t = h.shape[0]
    tm = PROJ_TM
    row = lambda i: (i, 0)
    return pl.pallas_call(
        _out_proj_body,
        grid=(t // tm,),
        in_specs=[
            pl.BlockSpec((tm, D_MODEL), row),
            pl.BlockSpec((tm, V_DIM), row),
            pl.BlockSpec((V_DIM, D_MODEL), lambda i: (0, 0)),
        ],
        out_specs=pl.BlockSpec((tm, D_MODEL), row),
        out_shape=jax.ShapeDtypeStruct((t, D_MODEL), F32),
        compiler_params=_params(("parallel",)),
        name="out_proj",
    )(h, o, w)


def _conv_body(h_ref, g_ref, win_ref, bin_ref, wdw_ref, bdw_ref, lng_ref, lnb_ref,
               wout_ref, bout_ref, out_ref, ubuf_ref, cbuf_ref, *, ts):
    n_lb = D_MODEL // LANES
    halo = CONV_HALO
    rows = CONV_ROWS
    first_tap_row = halo - (CONV_WIDTH - 1)

    @pl.when(pl.program_id(1) == 0)
    def _():
        ubuf_ref[:, 0:halo, :] = jnp.zeros((n_lb, halo, LANES), F32)

    x = h_ref[...]
    xn = _rms(x, g_ref[...]).astype(BF16)
    y = jnp.dot(xn, win_ref[...], preferred_element_type=F32) + bin_ref[...]
    u = y[:, :D_MODEL] * _sigmoid(y[:, D_MODEL:])
    for j in range(n_lb):
        ubuf_ref[j, halo:halo + ts, :] = u[:, j * LANES:(j + 1) * LANES]

    def lane_block(j, carry):
        for r in range(ts // rows):
            acc = jnp.zeros((rows // SUBLANES, SUBLANES, LANES), F32)
            for k in range(CONV_WIDTH):
                seg = ubuf_ref[j, pl.ds(r * rows + first_tap_row + k, rows), :]
                acc = acc + seg.reshape(rows // SUBLANES, SUBLANES, LANES) * wdw_ref[j, k][None]
            cbuf_ref[j, r * rows:(r + 1) * rows, :] = acc.reshape(rows, LANES)
        return carry

    lax.fori_loop(0, n_lb, lane_block, 0)

    ubuf_ref[:, 0:halo, :] = ubuf_ref[:, ts:ts + halo, :]

    c = jnp.concatenate([cbuf_ref[j] for j in range(n_lb)], axis=1) + bdw_ref[...]
    mu = jnp.mean(c, axis=-1, keepdims=True)
    d = c - mu
    var = jnp.mean(d * d, axis=-1, keepdims=True)
    z = d * lax.rsqrt(var + LN_EPS) * lng_ref[...] + lnb_ref[...]
    z = (z * _sigmoid(z)).astype(BF16)
    out_ref[...] = x + jnp.dot(z, wout_ref[...], preferred_element_type=F32) + bout_ref[...]


def _conv_layer(h, batch, seq, norm_g, w_in, b_in, w_dw, b_dw, ln_g, ln_b, w_out, b_out):
    ts = CONV_TS
    n_s = seq // ts
    n_lb = D_MODEL // LANES
    wdw = jnp.broadcast_to(
        w_dw.reshape(CONV_WIDTH, n_lb, 1, LANES).transpose(1, 0, 2, 3),
        (n_lb, CONV_WIDTH, SUBLANES, LANES))
    row = lambda b, s: (b * n_s + s, 0)
    fixed = lambda b, s: (0, 0)
    vec = pl.BlockSpec((1, D_MODEL), fixed)
    return pl.pallas_call(
        functools.partial(_conv_body, ts=ts),
        grid=(batch, n_s),
        in_specs=[
            pl.BlockSpec((ts, D_MODEL), row),
            vec,
            pl.BlockSpec((D_MODEL, 2 * D_MODEL), fixed),
            pl.BlockSpec((1, 2 * D_MODEL), fixed),
            pl.BlockSpec((n_lb, CONV_WIDTH, SUBLANES, LANES), lambda b, s: (0, 0, 0, 0)),
            vec, vec, vec,
            pl.BlockSpec((D_MODEL, D_MODEL), fixed),
            vec,
        ],
        out_specs=pl.BlockSpec((ts, D_MODEL), row),
        out_shape=jax.ShapeDtypeStruct(h.shape, F32),
        scratch_shapes=[
            pltpu.VMEM((n_lb, CONV_HALO + ts, LANES), F32),
            pltpu.VMEM((n_lb, ts, LANES), F32),
        ],
        compiler_params=_params(("parallel", "arbitrary")),
        name="conformer_conv",
    )(h, norm_g.reshape(1, -1), w_in, b_in.reshape(1, -1), wdw, b_dw.reshape(1, -1),
      ln_g.reshape(1, -1), ln_b.reshape(1, -1), w_out, b_out.reshape(1, -1))


def _bucket_tiles(t):
    assert t >= MAX_DISTANCE
    qp = np.arange(t)[:, None]
    kp = np.arange(t)[None, :]
    max_exact = NUM_BUCKETS // 2

    def bucket(n):
        nf = np.maximum(n, 1).astype(np.float32)
        large = max_exact + (np.log(nf / max_exact) / math.log(MAX_DISTANCE / max_exact)
                             * (NUM_BUCKETS - max_exact)).astype(np.int32)
        return np.where(n < max_exact, n, np.minimum(large, NUM_BUCKETS - 1))

    diag = np.where(kp <= qp, bucket(np.maximum(qp - kp, 0)), -1)
    left = bucket(qp + t - kp)
    return np.stack([diag, left]).astype(np.int32)


def _bias_body(rb_ref, bucket_ref, out_ref):
    h = pl.program_id(0)
    far = rb_ref[NUM_BUCKETS - 1, h]
    bk = bucket_ref[...]
    acc = jnp.full(bk.shape, NEG_INF, F32)
    for b in range(NUM_BUCKETS):
        acc = jnp.where(bk == b, rb_ref[b, h] - far, acc)
    out_ref[...] = acc


def _rel_bias_tiles(rel_bias, t):
    return pl.pallas_call(
        _bias_body,
        grid=(N_HEADS,),
        in_specs=[
            pl.BlockSpec(memory_space=pltpu.SMEM),
            pl.BlockSpec((2, t, t), lambda h: (0, 0, 0)),
        ],
        out_specs=pl.BlockSpec((None, 2, t, t), lambda h: (h, 0, 0, 0)),
        out_shape=jax.ShapeDtypeStruct((N_HEADS, 2, t, t), F32),
        compiler_params=_params(("parallel",)),
        name="rel_bias_tiles",
    )(rel_bias.astype(F32), jnp.asarray(_bucket_tiles(t)))


def _attn_body(q_ref, kt_ref, v_ref, bias_ref, lam_ref, sub_ref, out_ref, vaug_ref,
               *, t, n_q, lambda_init):
    seq = v_ref.shape[0]
    vaug_ref[:, :LANES] = v_ref[...]
    vaug_ref[:, LANES:] = jnp.ones((seq, LANES), BF16)
    lq = lam_ref[...]
    lam = (jnp.exp(jnp.sum(lq[0:1] * lq[1:2], axis=-1, keepdims=True))
           - jnp.exp(jnp.sum(lq[2:3] * lq[3:4], axis=-1, keepdims=True)) + lambda_init)

    for i in range(n_q):
        q = q_ref[i * t:(i + 1) * t, :]
        lane = lax.broadcasted_iota(jnp.int32, q.shape, 1)
        zero = jnp.zeros_like(q)
        q12 = jnp.concatenate([jnp.where(lane < HEAD_DIM, q, zero),
                               jnp.where(lane >= HEAD_DIM, q, zero)], axis=0)
        pieces = []
        if i >= 2:
            pieces.append((0, (i - 1) * t, None))
        if i >= 1:
            pieces.append(((i - 1) * t, i * t, 1))
        pieces.append((i * t, (i + 1) * t, 0))

        scores = []
        colmax = None
        for c0, c1, bias_idx in pieces:
            s = jnp.dot(q12, kt_ref[:, c0:c1], preferred_element_type=F32)
            if bias_idx is not None:
                s = (s.reshape(2, t, t) + bias_ref[bias_idx][None]).reshape(2 * t, t)
            for c in range((c1 - c0) // LANES):
                blk = s[:, c * LANES:(c + 1) * LANES]
                colmax = blk if colmax is None else jnp.maximum(colmax, blk)
            scores.append(s)
        m = jnp.max(colmax, axis=1, keepdims=True)

        acc = None
        for (c0, c1, _), s in zip(pieces, scores):
            p = jnp.exp(s - m).astype(BF16)
            d = jnp.dot(p, vaug_ref[c0:c1, :], preferred_element_type=F32)
            acc = d if acc is None else acc + d
        o12 = acc[:, :LANES] / acc[:, LANES:]
        o = o12[:t] - lam * o12[t:]
        out_ref[i * t:(i + 1) * t, :] = (
            _rms(o, sub_ref[...]) * (1.0 - lambda_init)).astype(out_ref.dtype)


def _attention(q, kt, v, bias_tiles, lam_rows, subln, batch, seq, lambda_init):
    t = ATT_T
    n_q = seq // t
    q3 = q.reshape(batch, seq, QK_DIM)
    head_cols = pl.BlockSpec((None, seq, LANES), lambda b, h: (b, 0, h))
    out = pl.pallas_call(
        functools.partial(_attn_body, t=t, n_q=n_q, lambda_init=lambda_init),
        grid=(batch, N_HEADS),
        in_specs=[
            head_cols,
            pl.BlockSpec((None, LANES, seq), lambda b, h: (b, h, 0)),
            head_cols,
            pl.BlockSpec((None, 2, t, t), lambda b, h: (h, 0, 0, 0)),
            pl.BlockSpec((4, HEAD_DIM), lambda b, h: (0, 0)),
            pl.BlockSpec((1, LANES), lambda b, h: (0, 0)),
        ],
        out_specs=head_cols,
        out_shape=jax.ShapeDtypeStruct((batch, seq, V_DIM), BF16),
        scratch_shapes=[pltpu.VMEM((seq, 2 * LANES), BF16)],
        compiler_params=_params(("parallel", "parallel")),
        name="diff_attention",
    )(q3, kt, v, bias_tiles, lam_rows, subln.reshape(1, LANES))
    return out.reshape(batch * seq, V_DIM)


def _kv_body(h_ref, g_ref, wkt_ref, wv_ref, kt_ref, v_ref):
    xn = _rms(h_ref[...], g_ref[...]).astype(BF16)
    kt = lax.dot_general(wkt_ref[...], xn, (((1,), (1,)), ((), ())),
                         preferred_element_type=F32)
    kt_ref[...] = kt.astype(BF16)
    v_ref[...] = jnp.dot(xn, wv_ref[...], preferred_element_type=F32).astype(BF16)


def _shared_kv(h, batch, seq, norm_g, w_kt, w_v):
    tm = PROJ_TM
    n_s = seq // tm
    fixed = lambda b, s: (0, 0)
    return pl.pallas_call(
        _kv_body,
        grid=(batch, n_s),
        in_specs=[
            pl.BlockSpec((tm, D_MODEL), lambda b, s: (b * n_s + s, 0)),
            pl.BlockSpec((1, D_MODEL), fixed),
            pl.BlockSpec((QK_DIM, D_MODEL), fixed),
            pl.BlockSpec((D_MODEL, V_DIM), fixed),
        ],
        out_specs=[
            pl.BlockSpec((None, QK_DIM, tm), lambda b, s: (b, 0, s)),
            pl.BlockSpec((None, tm, V_DIM), lambda b, s: (b, s, 0)),
        ],
        out_shape=[
            jax.ShapeDtypeStruct((batch, QK_DIM, seq), BF16),
            jax.ShapeDtypeStruct((batch, seq, V_DIM), BF16),
        ],
        compiler_params=_params(("parallel", "parallel")),
        name="shared_kv",
    )(h, norm_g.reshape(1, D_MODEL), w_kt, w_v)


def _head_major(w):
    d = w.shape[0]
    return w.reshape(d, 2, N_HEADS, HEAD_DIM).transpose(0, 2, 1, 3).reshape(d, QK_DIM)


def kernel(x, p, ffn1_norm, ffn1_w_in, ffn1_w_out, mix_norm, ffn2_norm, ffn2_w_in, ffn2_w_out,
           ple_norm, ple_w_gate, ple_w_proj, conv_w_in, conv_b_in, conv_w_dw, conv_b_dw,
           conv_ln_g, conv_ln_b, conv_w_out, conv_b_out, kv_norm, w_kv, attn_w_q,
           attn_lq1, attn_lk1, attn_lq2, attn_lk2, attn_subln, attn_w_o, rel_bias, final_norm):
    batch, seq, _ = x.shape
    tokens = batch * seq
    h = x.reshape(tokens, D_MODEL)
    p2 = p.reshape(DEPTH, tokens, PLE_DIM)
    bias_tiles = _rel_bias_tiles(rel_bias, ATT_T)
    w_kt = _head_major(w_kv[:, :QK_DIM]).T.astype(BF16)
    w_v = w_kv[:, QK_DIM:].astype(BF16)
    kt = v = None
    for i in range(DEPTH):
        if i == N_A:
            kt, v = _shared_kv(h, batch, seq, kv_norm, w_kt, w_v)
        h = _ffn(h, ffn1_norm[i], ffn1_w_in[i].astype(BF16), ffn1_w_out[i].astype(BF16))
        if i < N_A:
            h = _conv_layer(h, batch, seq, mix_norm[i], conv_w_in[i].astype(BF16), conv_b_in[i],
                            conv_w_dw[i], conv_b_dw[i], conv_ln_g[i], conv_ln_b[i],
                            conv_w_out[i].astype(BF16), conv_b_out[i])
        else:
            j = i - N_A
            lambda_init = 0.8 - 0.6 * math.exp(-0.3 * i)
            w_q = (_head_major(attn_w_q[j]) * (HEAD_DIM ** -0.5)).astype(BF16)
            (q,) = _norm_proj(h, mix_norm[i], w_q, 1)
            lam_rows = jnp.stack([attn_lq1[j], attn_lk1[j], attn_lq2[j], attn_lk2[j]]).astype(F32)
            o = _attention(q, kt, v, bias_tiles, lam_rows, attn_subln[j], batch, seq, lambda_init)
            h = _out_proj(h, o, attn_w_o[j].astype(BF16))
        h = _ffn(h, ffn2_norm[i], ffn2_w_in[i].astype(BF16), ffn2_w_out[i].astype(BF16),
                 ple=(ple_norm[i], ple_w_gate[i].astype(BF16), p2[i], ple_w_proj[i].astype(BF16)),
                 final_g=final_norm if i == DEPTH - 1 else None)
    return h.reshape(batch, seq, D_MODEL)
```

```python
import functools
import math

import jax
import jax.numpy as jnp
import numpy as np
from jax import lax
from jax.experimental import pallas as pl
from jax.experimental.pallas import tpu as pltpu

D_MODEL = 1024
DEPTH = 4
N_A = DEPTH // 2
D_FF = 4 * D_MODEL
CONV_WIDTH = 31
HEAD_DIM = 64
N_HEADS = D_MODEL // (2 * HEAD_DIM)
QK_DIM = 2 * N_HEADS * HEAD_DIM
V_DIM = N_HEADS * 2 * HEAD_DIM
NUM_BUCKETS = 32
MAX_DISTANCE = 128
PLE_DIM = 256
RMS_EPS = 1e-6
LN_EPS = 1e-5
NEG_INF = -1e30

LANES = 128
SUBLANES = 8
VMEM_LIMIT_BYTES = 56 * 1024 * 1024

FFN_TM = 1024
FFN_TF = 1024
FFN_SUB = 512
PROJ_TM = 1024
CONV_TS = 512
CONV_HALO = 32
CONV_ROWS = 32
ATT_T = 256

F32 = jnp.float32
BF16 = jnp.bfloat16


def _rms(x, g):
    ms = jnp.mean(x * x, axis=-1, keepdims=True)
    return x * lax.rsqrt(ms + RMS_EPS) * g


def _sigmoid(x):
    return 1.0 / (1.0 + jnp.exp(-x))


def _params(sem):
    return pltpu.CompilerParams(dimension_semantics=sem, vmem_limit_bytes=VMEM_LIMIT_BYTES)


def _ffn_body(*refs, n_f, ple, final):
    h_ref, g_ref, wg_ref, wu_ref, wo_ref = refs[:5]
    pos = 5
    if ple:
        pn_ref, pwg_ref, p_ref, pwp_ref = refs[pos:pos + 4]
        pos += 4
    if final:
        fn_ref = refs[pos]
        pos += 1
    out_ref, xn_ref, acc_ref = refs[pos:pos + 3]

    f = pl.program_id(1)

    @pl.when(f == 0)
    def _():
        xn_ref[...] = _rms(h_ref[...], g_ref[...]).astype(BF16)
        acc_ref[...] = jnp.zeros(acc_ref.shape, F32)

    xn = xn_ref[...]
    part = None
    for c in range(FFN_TF // FFN_SUB):
        cols = slice(c * FFN_SUB, (c + 1) * FFN_SUB)
        gate = jnp.dot(xn, wg_ref[:, cols], preferred_element_type=F32)
        up = jnp.dot(xn, wu_ref[:, cols], preferred_element_type=F32)
        act = (gate * _sigmoid(gate) * up).astype(BF16)
        d = jnp.dot(act, wo_ref[cols, :], preferred_element_type=F32)
        part = d if part is None else part + d
    acc_ref[...] += part

    @pl.when(f == n_f - 1)
    def _():
        hn = h_ref[...] + 0.5 * acc_ref[...]
        if ple:
            xg = _rms(hn, pn_ref[...]).astype(BF16)
            gt = _sigmoid(jnp.dot(xg, pwg_ref[...], preferred_element_type=F32))
            pr = jnp.dot(p_ref[...].astype(BF16), pwp_ref[...], preferred_element_type=F32)
            hn = hn + gt * pr
        if final:
            hn = _rms(hn, fn_ref[...])
        out_ref[...] = hn


def _ffn(h, norm_g, w_in, w_out, ple=None, final_g=None):
    t = h.shape[0]
    tm, tf = FFN_TM, FFN_TF
    n_f = D_FF // tf
    row = lambda i, f: (i, 0)
    fixed = lambda i, f: (0, 0)
    in_specs = [
        pl.BlockSpec((tm, D_MODEL), row),
        pl.BlockSpec((1, D_MODEL), fixed),
        pl.BlockSpec((D_MODEL, tf), lambda i, f: (0, f)),
        pl.BlockSpec((D_MODEL, tf), lambda i, f: (0, f + n_f)),
        pl.BlockSpec((tf, D_MODEL), lambda i, f: (f, 0)),
    ]
    args = [h, norm_g.reshape(1, D_MODEL), w_in, w_in, w_out]
    if ple is not None:
        pn, pwg, p, pwp = ple
        in_specs += [
            pl.BlockSpec((1, D_MODEL), fixed),
            pl.BlockSpec((D_MODEL, D_MODEL), fixed),
            pl.BlockSpec((tm, PLE_DIM), row),
            pl.BlockSpec((PLE_DIM, D_MODEL), fixed),
        ]
        args += [pn.reshape(1, D_MODEL), pwg, p, pwp]
    if final_g is not None:
        in_specs.append(pl.BlockSpec((1, D_MODEL), fixed))
        args.append(final_g.reshape(1, D_MODEL))
    body = functools.partial(_ffn_body, n_f=n_f, ple=ple is not None, final=final_g is not None)
    return pl.pallas_call(
        body,
        grid=(t // tm, n_f),
        in_specs=in_specs,
        out_specs=pl.BlockSpec((tm, D_MODEL), row),
        out_shape=jax.ShapeDtypeStruct((t, D_MODEL), F32),
        scratch_shapes=[pltpu.VMEM((tm, D_MODEL), BF16), pltpu.VMEM((tm, D_MODEL), F32)],
        compiler_params=_params(("parallel", "arbitrary")),
        name="ffn",
    )(*args)


def _norm_proj_body(h_ref, g_ref, w_ref, *out_refs):
    xn = _rms(h_ref[...], g_ref[...]).astype(BF16)
    y = jnp.dot(xn, w_ref[...], preferred_element_type=F32)
    for n, o_ref in enumerate(out_refs):
        o_ref[...] = y[:, n * D_MODEL:(n + 1) * D_MODEL].astype(o_ref.dtype)


def _norm_proj(h, norm_g, w, n_out):
    t = h.shape[0]
    tm = PROJ_TM
    row = lambda i: (i, 0)
    fixed = lambda i: (0, 0)
    return pl.pallas_call(
        _norm_proj_body,
        grid=(t // tm,),
        in_specs=[
            pl.BlockSpec((tm, D_MODEL), row),
            pl.BlockSpec((1, D_MODEL), fixed),
            pl.BlockSpec((D_MODEL, n_out * D_MODEL), fixed),
        ],
        out_specs=[pl.BlockSpec((tm, D_MODEL), row)] * n_out,
        out_shape=[jax.ShapeDtypeStruct((t, D_MODEL), BF16)] * n_out,
        compiler_params=_params(("parallel",)),
        name="norm_proj",
    )(h, norm_g.reshape(1, D_MODEL), w)


def _out_proj_body(h_ref, o_ref, w_ref, out_ref):
    out_ref[...] = h_ref[...] + jnp.dot(o_ref[...], w_ref[...], preferred_element_type=F32)


def _out_proj(h, o, w):
    t = h.shape[0]
    tm = PROJ_TM
    row = lambda i: (i, 0)
    return pl.pallas_call(
        _out_proj_body,
        grid=(t // tm,),
        in_specs=[
            pl.BlockSpec((tm, D_MODEL), row),
            pl.BlockSpec((tm, V_DIM), row),
            pl.BlockSpec((V_DIM, D_MODEL), lambda i: (0, 0)),
        ],
        out_specs=pl.BlockSpec((tm, D_MODEL), row),
        out_shape=jax.ShapeDtypeStruct((t, D_MODEL), F32),
        compiler_params=_params(("parallel",)),
        name="out_proj",
    )(h, o, w)


def _conv_body(h_ref, g_ref, win_ref, bin_ref, wdw_ref, bdw_ref, lng_ref, lnb_ref,
               wout_ref, bout_ref, out_ref, ubuf_ref, cbuf_ref, *, ts):
    n_lb = D_MODEL // LANES
    halo = CONV_HALO
    rows = CONV_ROWS
    first_tap_row = halo - (CONV_WIDTH - 1)

    @pl.when(pl.program_id(1) == 0)
    def _():
        ubuf_ref[:, 0:halo, :] = jnp.zeros((n_lb, halo, LANES), F32)

    x = h_ref[...]
    xn = _rms(x, g_ref[...]).astype(BF16)
    y = jnp.dot(xn, win_ref[...], preferred_element_type=F32) + bin_ref[...]
    u = y[:, :D_MODEL] * _sigmoid(y[:, D_MODEL:])
    for j in range(n_lb):
        ubuf_ref[j, halo:halo + ts, :] = u[:, j * LANES:(j + 1) * LANES]

    def lane_block(j, carry):
        for r in range(ts // rows):
            acc = jnp.zeros((rows // SUBLANES, SUBLANES, LANES), F32)
            for k in range(CONV_WIDTH):
                seg = ubuf_ref[j, pl.ds(r * rows + first_tap_row + k, rows), :]
                acc = acc + seg.reshape(rows // SUBLANES, SUBLANES, LANES) * wdw_ref[j, k][None]
            cbuf_ref[j, r * rows:(r + 1) * rows, :] = acc.reshape(rows, LANES)
        return carry

    lax.fori_loop(0, n_lb, lane_block, 0)

    ubuf_ref[:, 0:halo, :] = ubuf_ref[:, ts:ts + halo, :]

    c = jnp.concatenate([cbuf_ref[j] for j in range(n_lb)], axis=1) + bdw_ref[...]
    mu = jnp.mean(c, axis=-1, keepdims=True)
    d = c - mu
    var = jnp.mean(d * d, axis=-1, keepdims=True)
    z = d * lax.rsqrt(var + LN_EPS) * lng_ref[...] + lnb_ref[...]
    z = (z * _sigmoid(z)).astype(BF16)
    out_ref[...] = x + jnp.dot(z, wout_ref[...], preferred_element_type=F32) + bout_ref[...]


def _conv_layer(h, batch, seq, norm_g, w_in, b_in, w_dw, b_dw, ln_g, ln_b, w_out, b_out):
    ts = CONV_TS
    n_s = seq // ts
    n_lb = D_MODEL // LANES
    wdw = jnp.broadcast_to(
        w_dw.reshape(CONV_WIDTH, n_lb, 1, LANES).transpose(1, 0, 2, 3),
        (n_lb, CONV_WIDTH, SUBLANES, LANES))
    row = lambda b, s: (b * n_s + s, 0)
    fixed = lambda b, s: (0, 0)
    vec = pl.BlockSpec((1, D_MODEL), fixed)
    return pl.pallas_call(
        functools.partial(_conv_body, ts=ts),
        grid=(batch, n_s),
        in_specs=[
            pl.BlockSpec((ts, D_MODEL), row),
            vec,
            pl.BlockSpec((D_MODEL, 2 * D_MODEL), fixed),
            pl.BlockSpec((1, 2 * D_MODEL), fixed),
            pl.BlockSpec((n_lb, CONV_WIDTH, SUBLANES, LANES), lambda b, s: (0, 0, 0, 0)),
            vec, vec, vec,
            pl.BlockSpec((D_MODEL, D_MODEL), fixed),
            vec,
        ],
        out_specs=pl.BlockSpec((ts, D_MODEL), row),
        out_shape=jax.ShapeDtypeStruct(h.shape, F32),
        scratch_shapes=[
            pltpu.VMEM((n_lb, CONV_HALO + ts, LANES), F32),
            pltpu.VMEM((n_lb, ts, LANES), F32),
        ],
        compiler_params=_params(("parallel", "arbitrary")),
        name="conformer_conv",
    )(h, norm_g.reshape(1, -1), w_in, b_in.reshape(1, -1), wdw, b_dw.reshape(1, -1),
      ln_g.reshape(1, -1), ln_b.reshape(1, -1), w_out, b_out.reshape(1, -1))


def _bucket_tiles(t):
    assert t >= MAX_DISTANCE
    qp = np.arange(t)[:, None]
    kp = np.arange(t)[None, :]
    max_exact = NUM_BUCKETS // 2

    def bucket(n):
        nf = np.maximum(n, 1).astype(np.float32)
        large = max_exact + (np.log(nf / max_exact) / math.log(MAX_DISTANCE / max_exact)
                             * (NUM_BUCKETS - max_exact)).astype(np.int32)
        return np.where(n < max_exact, n, np.minimum(large, NUM_BUCKETS - 1))

    diag = np.where(kp <= qp, bucket(np.maximum(qp - kp, 0)), -1)
    left = bucket(qp + t - kp)
    return np.stack([diag, left]).astype(np.int32)


def _bias_body(rb_ref, bucket_ref, out_ref):
    h = pl.program_id(0)
    far = rb_ref[NUM_BUCKETS - 1, h]
    bk = bucket_ref[...]
    acc = jnp.full(bk.shape, NEG_INF, F32)
    for b in range(NUM_BUCKETS):
        acc = jnp.where(bk == b, rb_ref[b, h] - far, acc)
    out_ref[...] = acc


def _rel_bias_tiles(rel_bias, t):
    return pl.pallas_call(
        _bias_body,
        grid=(N_HEADS,),
        in_specs=[
            pl.BlockSpec(memory_space=pltpu.SMEM),
            pl.BlockSpec((2, t, t), lambda h: (0, 0, 0)),
        ],
        out_specs=pl.BlockSpec((None, 2, t, t), lambda h: (h, 0, 0, 0)),
        out_shape=jax.ShapeDtypeStruct((N_HEADS, 2, t, t), F32),
        compiler_params=_params(("parallel",)),
        name="rel_bias_tiles",
    )(rel_bias.astype(F32), jnp.asarray(_bucket_tiles(t)))


def _attn_body(q_ref, kt_ref, v_ref, bias_ref, lam_ref, sub_ref, out_ref, vaug_ref,
               *, t, n_q, lambda_init):
    seq = v_ref.shape[0]
    vaug_ref[:, :LANES] = v_ref[...]
    vaug_ref[:, LANES:] = jnp.ones((seq, LANES), BF16)
    lq = lam_ref[...]
    lam = (jnp.exp(jnp.sum(lq[0:1] * lq[1:2], axis=-1, keepdims=True))
           - jnp.exp(jnp.sum(lq[2:3] * lq[3:4], axis=-1, keepdims=True)) + lambda_init)

    for i in range(n_q):
        q = q_ref[i * t:(i + 1) * t, :]
        lane = lax.broadcasted_iota(jnp.int32, q.shape, 1)
        zero = jnp.zeros_like(q)
        q12 = jnp.concatenate([jnp.where(lane < HEAD_DIM, q, zero),
                               jnp.where(lane >= HEAD_DIM, q, zero)], axis=0)
        pieces = []
        if i >= 2:
            pieces.append((0, (i - 1) * t, None))
        if i >= 1:
            pieces.append(((i - 1) * t, i * t, 1))
        pieces.append((i * t, (i + 1) * t, 0))

        scores = []
        colmax = None
        for c0, c1, bias_idx in pieces:
            s = jnp.dot(q12, kt_ref[:, c0:c1], preferred_element_type=F32)
            if bias_idx is not None:
                s = (s.reshape(2, t, t) + bias_ref[bias_idx][None]).reshape(2 * t, t)
            for c in range((c1 - c0) // LANES):
                blk = s[:, c * LANES:(c + 1) * LANES]
                colmax = blk if colmax is None else jnp.maximum(colmax, blk)
            scores.append(s)
        m = jnp.max(colmax, axis=1, keepdims=True)

        acc = None
        for (c0, c1, _), s in zip(pieces, scores):
            p = jnp.exp(s - m).astype(BF16)
            d = jnp.dot(p, vaug_ref[c0:c1, :], preferred_element_type=F32)
            acc = d if acc is None else acc + d
        o12 = acc[:, :LANES] / acc[:, LANES:]
        o = o12[:t] - lam * o12[t:]
        out_ref[i * t:(i + 1) * t, :] = (
            _rms(o, sub_ref[...]) * (1.0 - lambda_init)).astype(out_ref.dtype)


def _attention(q, kt, v, bias_tiles, lam_rows, subln, batch, seq, lambda_init):
    t = ATT_T
    n_q = seq // t
    q3 = q.reshape(batch, seq, QK_DIM)
    head_cols = pl.BlockSpec((None, seq, LANES), lambda b, h: (b, 0, h))
    out = pl.pallas_call(
        functools.partial(_attn_body, t=t, n_q=n_q, lambda_init=lambda_init),
        grid=(batch, N_HEADS),
        in_specs=[
            head_cols,
            pl.BlockSpec((None, LANES, seq), lambda b, h: (b, h, 0)),
            head_cols,
            pl.BlockSpec((None, 2, t, t), lambda b, h: (h, 0, 0, 0)),
            pl.BlockSpec((4, HEAD_DIM), lambda b, h: (0, 0)),
            pl.BlockSpec((1, LANES), lambda b, h: (0, 0)),
        ],
        out_specs=head_cols,
        out_shape=jax.ShapeDtypeStruct((batch, seq, V_DIM), BF16),
        scratch_shapes=[pltpu.VMEM((seq, 2 * LANES), BF16)],
        compiler_params=_params(("parallel", "parallel")),
        name="diff_attention",
    )(q3, kt, v, bias_tiles, lam_rows, subln.reshape(1, LANES))
    return out.reshape(batch * seq, V_DIM)


def _kv_body(h_ref, g_ref, wkt_ref, wv_ref, kt_ref, v_ref):
    xn = _rms(h_ref[...], g_ref[...]).astype(BF16)
    kt = lax.dot_general(wkt_ref[...], xn, (((1,), (1,)), ((), ())),
                         preferred_element_type=F32)
    kt_ref[...] = kt.astype(BF16)
    v_ref[...] = jnp.dot(xn, wv_ref[...], preferred_element_type=F32).astype(BF16)


def _shared_kv(h, batch, seq, norm_g, w_kt, w_v):
    tm = PROJ_TM
    n_s = seq // tm
    fixed = lambda b, s: (0, 0)
    return pl.pallas_call(
        _kv_body,
        grid=(batch, n_s),
        in_specs=[
            pl.BlockSpec((tm, D_MODEL), lambda b, s: (b * n_s + s, 0)),
            pl.BlockSpec((1, D_MODEL), fixed),
            pl.BlockSpec((QK_DIM, D_MODEL), fixed),
            pl.BlockSpec((D_MODEL, V_DIM), fixed),
        ],
        out_specs=[
            pl.BlockSpec((None, QK_DIM, tm), lambda b, s: (b, 0, s)),
            pl.BlockSpec((None, tm, V_DIM), lambda b, s: (b, s, 0)),
        ],
        out_shape=[
            jax.ShapeDtypeStruct((batch, QK_DIM, seq), BF16),
            jax.ShapeDtypeStruct((batch, seq, V_DIM), BF16),
        ],
        compiler_params=_params(("parallel", "parallel")),
        name="shared_kv",
    )(h, norm_g.reshape(1, D_MODEL), w_kt, w_v)


def _head_major(w):
    d = w.shape[0]
    return w.reshape(d, 2, N_HEADS, HEAD_DIM).transpose(0, 2, 1, 3).reshape(d, QK_DIM)


def kernel(x, p, ffn1_norm, ffn1_w_in, ffn1_w_out, mix_norm, ffn2_norm, ffn2_w_in, ffn2_w_out,
           ple_norm, ple_w_gate, ple_w_proj, conv_w_in, conv_b_in, conv_w_dw, conv_b_dw,
           conv_ln_g, conv_ln_b, conv_w_out, conv_b_out, kv_norm, w_kv, attn_w_q,
           attn_lq1, attn_lk1, attn_lq2, attn_lk2, attn_subln, attn_w_o, rel_bias, final_norm):
    batch, seq, _ = x.shape
    tokens = batch * seq
    h = x.reshape(tokens, D_MODEL)
    p2 = p.reshape(DEPTH, tokens, PLE_DIM)
    bias_tiles = _rel_bias_tiles(rel_bias, ATT_T)
    w_kt = _head_major(w_kv[:, :QK_DIM]).T.astype(BF16)
    w_v = w_kv[:, QK_DIM:].astype(BF16)
    kt = v = None
    for i in range(DEPTH):
        if i == N_A:
            kt, v = _shared_kv(h, batch, seq, kv_norm, w_kt, w_v)
        h = _ffn(h, ffn1_norm[i], ffn1_w_in[i].astype(BF16), ffn1_w_out[i].astype(BF16))
        if i < N_A:
            h = _conv_layer(h, batch, seq, mix_norm[i], conv_w_in[i].astype(BF16), conv_b_in[i],
                            conv_w_dw[i], conv_b_dw[i], conv_ln_g[i], conv_ln_b[i],
                            conv_w_out[i].astype(BF16), conv_b_out[i])
        else:
            j = i - N_A
            lambda_init = 0.8 - 0.6 * math.exp(-0.3 * i)
            w_q = (_head_major(attn_w_q[j]) * (HEAD_DIM ** -0.5)).astype(BF16)
            (q,) = _norm_proj(h, mix_norm[i], w_q, 1)
            lam_rows = jnp.stack([attn_lq1[j], attn_lk1[j], attn_lq2[j], attn_lk2[j]]).astype(F32)
            o = _attention(q, kt, v, bias_tiles, lam_rows, attn_subln[j], batch, seq, lambda_init)
            h = _out_proj(h, o, attn_w_o[j].astype(BF16))
        h = _ffn(h, ffn2_norm[i], ffn2_w_in[i].astype(BF16), ffn2_w_out[i].astype(BF16),
                 ple=(ple_norm[i], ple_w_gate[i].astype(BF16), p2[i], ple_w_proj[i].astype(BF16)),
                 final_g=final_norm if i == DEPTH - 1 else None)
    return h.reshape(batch, seq, D_MODEL)
```

```python
import functools
import math

import jax
import jax.numpy as jnp
import numpy as np
from jax import lax
from jax.experimental import pallas as pl
from jax.experimental.pallas import tpu as pltpu

D_MODEL = 1024
DEPTH = 4
N_A = DEPTH // 2
D_FF = 4 * D_MODEL
CONV_WIDTH = 31
HEAD_DIM = 64
N_HEADS = D_MODEL // (2 * HEAD_DIM)
QK_DIM = 2 * N_HEADS * HEAD_DIM
V_DIM = N_HEADS * 2 * HEAD_DIM
NUM_BUCKETS = 32
MAX_DISTANCE = 128
PLE_DIM = 256
RMS_EPS = 1e-6
LN_EPS = 1e-5
NEG_INF = -1e30

LANES = 128
SUBLANES = 8
VMEM_LIMIT_BYTES = 56 * 1024 * 1024

FFN_TM = 1024
FFN_TF = 1024
FFN_SUB = 512
PROJ_TM = 1024
CONV_TS = 512
CONV_HALO = 32
CONV_ROWS = 32
ATT_T = 256

F32 = jnp.float32
BF16 = jnp.bfloat16


def _rms(x, g):
    ms = jnp.mean(x * x, axis=-1, keepdims=True)
    return x * lax.rsqrt(ms + RMS_EPS) * g


def _sigmoid(x):
    return 1.0 / (1.0 + jnp.exp(-x))


def _params(sem):
    return pltpu.CompilerParams(dimension_semantics=sem, vmem_limit_bytes=VMEM_LIMIT_BYTES)


def _ffn_body(*refs, n_f, ple, final):
    h_ref, g_ref, wg_ref, wu_ref, wo_ref = refs[:5]
    pos = 5
    if ple:
        pn_ref, pwg_ref, p_ref, pwp_ref = refs[pos:pos + 4]
        pos += 4
    if final:
        fn_ref = refs[pos]
        pos += 1
    out_ref, xn_ref, acc_ref = refs[pos:pos + 3]

    f = pl.program_id(1)

    @pl.when(f == 0)
    def _():
        xn_ref[...] = _rms(h_ref[...], g_ref[...]).astype(BF16)
        acc_ref[...] = jnp.zeros(acc_ref.shape, F32)

    xn = xn_ref[...]
    part = None
    for c in range(FFN_TF // FFN_SUB):
        cols = slice(c * FFN_SUB, (c + 1) * FFN_SUB)
        gate = jnp.dot(xn, wg_ref[:, cols], preferred_element_type=F32)
        up = jnp.dot(xn, wu_ref[:, cols], preferred_element_type=F32)
        act = (gate * _sigmoid(gate) * up).astype(BF16)
        d = jnp.dot(act, wo_ref[cols, :], preferred_element_type=F32)
        part = d if part is None else part + d
    acc_ref[...] += part

    @pl.when(f == n_f - 1)
    def _():
        hn = h_ref[...] + 0.5 * acc_ref[...]
        if ple:
            xg = _rms(hn, pn_ref[...]).astype(BF16)
            gt = _sigmoid(jnp.dot(xg, pwg_ref[...], preferred_element_type=F32))
            pr = jnp.dot(p_ref[...].astype(BF16), pwp_ref[...], preferred_element_type=F32)
            hn = hn + gt * pr
        if final:
            hn = _rms(hn, fn_ref[...])
        out_ref[...] = hn


def _ffn(h, layer, norm_g, w_in, w_out, ple=None, final_g=None):
    t = h.shape[0]
    tm, tf = FFN_TM, FFN_TF
    n_f = D_FF // tf
    row = lambda i, f: (i, 0)
    fixed = lambda i, f: (0, 0)
    in_specs = [
        pl.BlockSpec((tm, D_MODEL), row),
        pl.BlockSpec((1, D_MODEL), fixed),
        pl.BlockSpec((None, D_MODEL, tf), lambda i, f: (layer, 0, f)),
        pl.BlockSpec((None, D_MODEL, tf), lambda i, f: (layer, 0, f + n_f)),
        pl.BlockSpec((None, tf, D_MODEL), lambda i, f: (layer, f, 0)),
    ]
    args = [h, norm_g.reshape(1, D_MODEL), w_in, w_in, w_out]
    if ple is not None:
        pn, pwg, p, pwp = ple
        in_specs += [
            pl.BlockSpec((1, D_MODEL), fixed),
            pl.BlockSpec((None, D_MODEL, D_MODEL), lambda i, f: (layer, 0, 0)),
            pl.BlockSpec((None, tm, PLE_DIM), lambda i, f: (layer, i, 0)),
            pl.BlockSpec((None, PLE_DIM, D_MODEL), lambda i, f: (layer, 0, 0)),
        ]
        args += [pn.reshape(1, D_MODEL), pwg, p, pwp]
    if final_g is not None:
        in_specs.append(pl.BlockSpec((1, D_MODEL), fixed))
        args.append(final_g.reshape(1, D_MODEL))
    body = functools.partial(_ffn_body, n_f=n_f, ple=ple is not None, final=final_g is not None)
    return pl.pallas_call(
        body,
        grid=(t // tm, n_f),
        in_specs=in_specs,
        out_specs=pl.BlockSpec((tm, D_MODEL), row),
        out_shape=jax.ShapeDtypeStruct((t, D_MODEL), F32),
        scratch_shapes=[pltpu.VMEM((tm, D_MODEL), BF16), pltpu.VMEM((tm, D_MODEL), F32)],
        compiler_params=_params(("parallel", "arbitrary")),
        name="ffn",
    )(*args)


def _norm_proj_body(h_ref, g_ref, w_ref, *out_refs):
    xn = _rms(h_ref[...], g_ref[...]).astype(BF16)
    y = jnp.dot(xn, w_ref[...], preferred_element_type=F32)
    for n, o_ref in enumerate(out_refs):
        o_ref[...] = y[:, n * D_MODEL:(n + 1) * D_MODEL].astype(o_ref.dtype)


def _norm_proj(h, norm_g, w, n_out):
    t = h.shape[0]
    tm = PROJ_TM
    row = lambda i: (i, 0)
    fixed = lambda i: (0, 0)
    return pl.pallas_call(
        _norm_proj_body,
        grid=(t // tm,),
        in_specs=[
            pl.BlockSpec((tm, D_MODEL), row),
            pl.BlockSpec((1, D_MODEL), fixed),
            pl.BlockSpec((D_MODEL, n_out * D_MODEL), fixed),
        ],
        out_specs=[pl.BlockSpec((tm, D_MODEL), row)] * n_out,
        out_shape=[jax.ShapeDtypeStruct((t, D_MODEL), BF16)] * n_out,
        compiler_params=_params(("parallel",)),
        name="norm_proj",
    )(h, norm_g.reshape(1, D_MODEL), w)


def _out_proj_body(h_ref, o_ref, w_ref, out_ref):
    out_ref[...] = h_ref[...] + jnp.dot(o_ref[...], w_ref[...], preferred_element_type=F32)


def _out_proj(h, o, layer, w):
    t = h.shape[0]
    tm = PROJ_TM
    row = lambda i: (i, 0)
    return pl.pallas_call(
        _out_proj_body,
        grid=(t // tm,),
        in_specs=[
            pl.BlockSpec((tm, D_MODEL), row),
            pl.BlockSpec((tm, V_DIM), row),
            pl.BlockSpec((None, V_DIM, D_MODEL), lambda i: (layer, 0, 0)),
        ],
        out_specs=pl.BlockSpec((tm, D_MODEL), row),
        out_shape=jax.ShapeDtypeStruct((t, D_MODEL), F32),
        compiler_params=_params(("parallel",)),
        name="out_proj",
    )(h, o, w)


def _conv_body(h_ref, g_ref, win_ref, bin_ref, wdw_ref, bdw_ref, lng_ref, lnb_ref,
               wout_ref, bout_ref, out_ref, ubuf_ref, cbuf_ref, *, ts):
    n_lb = D_MODEL // LANES
    halo = CONV_HALO
    rows = CONV_ROWS
    first_tap_row = halo - (CONV_WIDTH - 1)

    @pl.when(pl.program_id(1) == 0)
    def _():
        ubuf_ref[:, 0:halo, :] = jnp.zeros((n_lb, halo, LANES), F32)

    x = h_ref[...]
    xn = _rms(x, g_ref[...]).astype(BF16)
    y = jnp.dot(xn, win_ref[...], preferred_element_type=F32) + bin_ref[...]
    u = y[:, :D_MODEL] * _sigmoid(y[:, D_MODEL:])
    for j in range(n_lb):
        ubuf_ref[j, halo:halo + ts, :] = u[:, j * LANES:(j + 1) * LANES]

    def lane_block(j, carry):
        for r in range(ts // rows):
            acc = jnp.zeros((rows // SUBLANES, SUBLANES, LANES), F32)
            for k in range(CONV_WIDTH):
                seg = ubuf_ref[j, pl.ds(r * rows + first_tap_row + k, rows), :]
                acc = acc + seg.reshape(rows // SUBLANES, SUBLANES, LANES) * wdw_ref[j, k][None]
            cbuf_ref[j, r * rows:(r + 1) * rows, :] = acc.reshape(rows, LANES)
        return carry

    lax.fori_loop(0, n_lb, lane_block, 0)

    ubuf_ref[:, 0:halo, :] = ubuf_ref[:, ts:ts + halo, :]

    c = jnp.concatenate([cbuf_ref[j] for j in range(n_lb)], axis=1) + bdw_ref[...]
    mu = jnp.mean(c, axis=-1, keepdims=True)
    d = c - mu
    var = jnp.mean(d * d, axis=-1, keepdims=True)
    z = d * lax.rsqrt(var + LN_EPS) * lng_ref[...] + lnb_ref[...]
    z = (z * _sigmoid(z)).astype(BF16)
    out_ref[...] = x + jnp.dot(z, wout_ref[...], preferred_element_type=F32) + bout_ref[...]


def _conv_layer(h, batch, seq, layer, norm_g, w_in, b_in, w_dw, b_dw, ln_g, ln_b, w_out, b_out):
    ts = CONV_TS
    n_s = seq // ts
    n_lb = D_MODEL // LANES
    wdw = jnp.broadcast_to(
        w_dw.reshape(CONV_WIDTH, n_lb, 1, LANES).transpose(1, 0, 2, 3),
        (n_lb, CONV_WIDTH, SUBLANES, LANES))
    row = lambda b, s: (b * n_s + s, 0)
    fixed = lambda b, s: (0, 0)
    vec = pl.BlockSpec((1, D_MODEL), fixed)
    return pl.pallas_call(
        functools.partial(_conv_body, ts=ts),
        grid=(batch, n_s),
        in_specs=[
            pl.BlockSpec((ts, D_MODEL), row),
            vec,
            pl.BlockSpec((None, D_MODEL, 2 * D_MODEL), lambda b, s: (layer, 0, 0)),
            pl.BlockSpec((1, 2 * D_MODEL), fixed),
            pl.BlockSpec((n_lb, CONV_WIDTH, SUBLANES, LANES), lambda b, s: (0, 0, 0, 0)),
            vec, vec, vec,
            pl.BlockSpec((None, D_MODEL, D_MODEL), lambda b, s: (layer, 0, 0)),
            vec,
        ],
        out_specs=pl.BlockSpec((ts, D_MODEL), row),
        out_shape=jax.ShapeDtypeStruct(h.shape, F32),
        scratch_shapes=[
            pltpu.VMEM((n_lb, CONV_HALO + ts, LANES), F32),
            pltpu.VMEM((n_lb, ts, LANES), F32),
        ],
        compiler_params=_params(("parallel", "arbitrary")),
        name="conformer_conv",
    )(h, norm_g.reshape(1, -1), w_in, b_in.reshape(1, -1), wdw, b_dw.reshape(1, -1),
      ln_g.reshape(1, -1), ln_b.reshape(1, -1), w_out, b_out.reshape(1, -1))


def _bucket_tiles(t):
    assert t >= MAX_DISTANCE
    qp = np.arange(t)[:, None]
    kp = np.arange(t)[None, :]
    max_exact = NUM_BUCKETS // 2

    def bucket(n):
        nf = np.maximum(n, 1).astype(np.float32)
        large = max_exact + (np.log(nf / max_exact) / math.log(MAX_DISTANCE / max_exact)
                             * (NUM_BUCKETS - max_exact)).astype(np.int32)
        return np.where(n < max_exact, n, np.minimum(large, NUM_BUCKETS - 1))

    diag = np.where(kp <= qp, bucket(np.maximum(qp - kp, 0)), -1)
    left = bucket(qp + t - kp)
    return np.stack([diag, left]).astype(np.int32)


def _bias_body(rb_ref, bucket_ref, out_ref):
    h = pl.program_id(0)
    far = rb_ref[NUM_BUCKETS - 1, h]
    bk = bucket_ref[...]
    acc = jnp.full(bk.shape, NEG_INF, F32)
    for b in range(NUM_BUCKETS):
        acc = jnp.where(bk == b, rb_ref[b, h] - far, acc)
    out_ref[...] = acc


def _rel_bias_tiles(rel_bias, t):
    return pl.pallas_call(
        _bias_body,
        grid=(N_HEADS,),
        in_specs=[
            pl.BlockSpec(memory_space=pltpu.SMEM),
            pl.BlockSpec((2, t, t), lambda h: (0, 0, 0)),
        ],
        out_specs=pl.BlockSpec((None, 2, t, t), lambda h: (h, 0, 0, 0)),
        out_shape=jax.ShapeDtypeStruct((N_HEADS, 2, t, t), F32),
        compiler_params=_params(("parallel",)),
        name="rel_bias_tiles",
    )(rel_bias.astype(F32), jnp.asarray(_bucket_tiles(t)))


def _attn_body(q_ref, kt_ref, v_ref, bias_ref, lam_ref, sub_ref, out_ref, vaug_ref,
               *, t, n_q, lambda_init):
    seq = v_ref.shape[0]
    vaug_ref[:, :LANES] = v_ref[...]
    vaug_ref[:, LANES:] = jnp.ones((seq, LANES), BF16)
    lq = lam_ref[...]
    lam = (jnp.exp(jnp.sum(lq[0:1] * lq[1:2], axis=-1, keepdims=True))
           - jnp.exp(jnp.sum(lq[2:3] * lq[3:4], axis=-1, keepdims=True)) + lambda_init)

    for i in reversed(range(n_q)):
        q = q_ref[i * t:(i + 1) * t, :]
        lane = lax.broadcasted_iota(jnp.int32, q.shape, 1)
        zero = jnp.zeros_like(q)
        q12 = jnp.concatenate([jnp.where(lane < HEAD_DIM, q, zero),
                               jnp.where(lane >= HEAD_DIM, q, zero)], axis=0)
        pieces = []
        if i >= 2:
            pieces.append((0, (i - 1) * t, None))
        if i >= 1:
            pieces.append(((i - 1) * t, i * t, 1))
        pieces.append((i * t, (i + 1) * t, 0))

        scores = []
        colmax = None
        for c0, c1, bias_idx in pieces:
            s = jnp.dot(q12, kt_ref[:, c0:c1], preferred_element_type=F32)
            if bias_idx is not None:
                s = (s.reshape(2, t, t) + bias_ref[bias_idx][None]).reshape(2 * t, t)
            for c in range((c1 - c0) // LANES):
                blk = s[:, c * LANES:(c + 1) * LANES]
                colmax = blk if colmax is None else jnp.maximum(colmax, blk)
            scores.append(s)
        m = jnp.max(colmax, axis=1, keepdims=True)

        acc = None
        for (c0, c1, _), s in zip(pieces, scores):
            p = jnp.exp(s - m).astype(BF16)
            d = jnp.dot(p, vaug_ref[c0:c1, :], preferred_element_type=F32)
            acc = d if acc is None else acc + d
        o12 = acc[:, :LANES] / acc[:, LANES:]
        o = o12[:t] - lam * o12[t:]
        out_ref[i * t:(i + 1) * t, :] = (
            _rms(o, sub_ref[...]) * (1.0 - lambda_init)).astype(out_ref.dtype)


def _attention(q, kt, v, bias_tiles, lam_rows, subln, batch, seq, lambda_init):
    t = ATT_T
    n_q = seq // t
    q3 = q.reshape(batch, seq, QK_DIM)
    head_cols = pl.BlockSpec((None, seq, LANES), lambda b, h: (b, 0, h))
    out = pl.pallas_call(
        functools.partial(_attn_body, t=t, n_q=n_q, lambda_init=lambda_init),
        grid=(batch, N_HEADS),
        in_specs=[
            head_cols,
            pl.BlockSpec((None, LANES, seq), lambda b, h: (b, h, 0)),
            head_cols,
            pl.BlockSpec((None, 2, t, t), lambda b, h: (h, 0, 0, 0)),
            pl.BlockSpec((4, HEAD_DIM), lambda b, h: (0, 0)),
            pl.BlockSpec((1, LANES), lambda b, h: (0, 0)),
        ],
        out_specs=head_cols,
        out_shape=jax.ShapeDtypeStruct((batch, seq, V_DIM), BF16),
        scratch_shapes=[pltpu.VMEM((seq, 2 * LANES), BF16)],
        compiler_params=_params(("parallel", "parallel")),
        name="diff_attention",
    )(q3, kt, v, bias_tiles, lam_rows, subln.reshape(1, LANES))
    return out.reshape(batch * seq, V_DIM)


def _kv_body(h_ref, g_ref, wkt_ref, wv_ref, kt_ref, v_ref):
    xn = _rms(h_ref[...], g_ref[...]).astype(BF16)
    kt = lax.dot_general(wkt_ref[...], xn, (((1,), (1,)), ((), ())),
                         preferred_element_type=F32)
    kt_ref[...] = kt.astype(BF16)
    v_ref[...] = jnp.dot(xn, wv_ref[...], preferred_element_type=F32).astype(BF16)


def _shared_kv(h, batch, seq, norm_g, w_kt, w_v):
    tm = PROJ_TM
    n_s = seq // tm
    fixed = lambda b, s: (0, 0)
    return pl.pallas_call(
        _kv_body,
        grid=(batch, n_s),
        in_specs=[
            pl.BlockSpec((tm, D_MODEL), lambda b, s: (b * n_s + s, 0)),
            pl.BlockSpec((1, D_MODEL), fixed),
            pl.BlockSpec((QK_DIM, D_MODEL), fixed),
            pl.BlockSpec((D_MODEL, V_DIM), fixed),
        ],
        out_specs=[
            pl.BlockSpec((None, QK_DIM, tm), lambda b, s: (b, 0, s)),
            pl.BlockSpec((None, tm, V_DIM), lambda b, s: (b, s, 0)),
        ],
        out_shape=[
            jax.ShapeDtypeStruct((batch, QK_DIM, seq), BF16),
            jax.ShapeDtypeStruct((batch, seq, V_DIM), BF16),
        ],
        compiler_params=_params(("parallel", "parallel")),
        name="shared_kv",
    )(h, norm_g.reshape(1, D_MODEL), w_kt, w_v)


def _head_major(w):
    d = w.shape[0]
    return w.reshape(d, 2, N_HEADS, HEAD_DIM).transpose(0, 2, 1, 3).reshape(d, QK_DIM)


def kernel(x, p, ffn1_norm, ffn1_w_in, ffn1_w_out, mix_norm, ffn2_norm, ffn2_w_in, ffn2_w_out,
           ple_norm, ple_w_gate, ple_w_proj, conv_w_in, conv_b_in, conv_w_dw, conv_b_dw,
           conv_ln_g, conv_ln_b, conv_w_out, conv_b_out, kv_norm, w_kv, attn_w_q,
           attn_lq1, attn_lk1, attn_lq2, attn_lk2, attn_subln, attn_w_o, rel_bias, final_norm):
    batch, seq, _ = x.shape
    tokens = batch * seq
    h = x.reshape(tokens, D_MODEL)
    p2 = p.reshape(DEPTH, tokens, PLE_DIM)
    bias_tiles = _rel_bias_tiles(rel_bias, ATT_T)
    w_kt = _head_major(w_kv[:, :QK_DIM]).T.astype(BF16)
    w_v = w_kv[:, QK_DIM:].astype(BF16)
    ffn1_w_in, ffn1_w_out = ffn1_w_in.astype(BF16), ffn1_w_out.astype(BF16)
    ffn2_w_in, ffn2_w_out = ffn2_w_in.astype(BF16), ffn2_w_out.astype(BF16)
    ple_w_gate, ple_w_proj = ple_w_gate.astype(BF16), ple_w_proj.astype(BF16)
    conv_w_in, conv_w_out = conv_w_in.astype(BF16), conv_w_out.astype(BF16)
    attn_w_o = attn_w_o.astype(BF16)
    kt = v = None
    for i in range(DEPTH):
        if i == N_A:
            kt, v = _shared_kv(h, batch, seq, kv_norm, w_kt, w_v)
        h = _ffn(h, i, ffn1_norm[i], ffn1_w_in, ffn1_w_out)
        if i < N_A:
            h = _conv_layer(h, batch, seq, i, mix_norm[i], conv_w_in, conv_b_in[i],
                            conv_w_dw[i], conv_b_dw[i], conv_ln_g[i], conv_ln_b[i],
                            conv_w_out, conv_b_out[i])
        else:
            j = i - N_A
            lambda_init = 0.8 - 0.6 * math.exp(-0.3 * i)
            w_q = (_head_major(attn_w_q[j]) * (HEAD_DIM ** -0.5)).astype(BF16)
            (q,) = _norm_proj(h, mix_norm[i], w_q, 1)
            lam_rows = jnp.stack([attn_lq1[j], attn_lk1[j], attn_lq2[j], attn_lk2[j]]).astype(F32)
            o = _attention(q, kt, v, bias_tiles, lam_rows, attn_subln[j], batch, seq, lambda_init)
            h = _out_proj(h, o, j, attn_w_o)
        h = _ffn(h, i, ffn2_norm[i], ffn2_w_in, ffn2_w_out,
                 ple=(ple_norm[i], ple_w_gate, p2, ple_w_proj),
                 final_g=final_norm if i == DEPTH - 1 else None)
    return h.reshape(batch, seq, D_MODEL)
```

```python
import functools
import math

import jax
import jax.numpy as jnp
import numpy as np
from jax import lax
from jax.experimental import pallas as pl
from jax.experimental.pallas import tpu as pltpu

D_MODEL = 1024
DEPTH = 4
N_A = DEPTH // 2
D_FF = 4 * D_MODEL
CONV_WIDTH = 31
HEAD_DIM = 64
N_HEADS = D_MODEL // (2 * HEAD_DIM)
QK_DIM = 2 * N_HEADS * HEAD_DIM
V_DIM = N_HEADS * 2 * HEAD_DIM
NUM_BUCKETS = 32
MAX_DISTANCE = 128
PLE_DIM = 256
RMS_EPS = 1e-6
LN_EPS = 1e-5
NEG_INF = -1e30

LANES = 128
SUBLANES = 8
VMEM_LIMIT_BYTES = 56 * 1024 * 1024

FFN_TM = 1024
FFN_TF = 1024
FFN_SUB = 512
PROJ_TM = 1024
CONV_TS = 512
CONV_HALO = 32
CONV_ROWS = 32
ATT_T = 256

F32 = jnp.float32
BF16 = jnp.bfloat16


def _rms(x, g):
    ms = jnp.mean(x * x, axis=-1, keepdims=True)
    return x * lax.rsqrt(ms + RMS_EPS) * g


def _sigmoid(x):
    return 1.0 / (1.0 + jnp.exp(-x))


def _params(sem):
    return pltpu.CompilerParams(dimension_semantics=sem, vmem_limit_bytes=VMEM_LIMIT_BYTES)


def _ffn_body(*refs, n_f, attn_in, ple, final, q_out, cast):
    it = iter(refs)
    h_ref, g_ref, wg_ref, wu_ref, wo_ref = (next(it) for _ in range(5))
    if attn_in:
        o_ref, wproj_ref = next(it), next(it)
    if ple:
        pn_ref, pwg_ref, p_ref, pwp_ref = (next(it) for _ in range(4))
    if final:
        fn_ref = next(it)
    if q_out:
        qg_ref, wq_ref = next(it), next(it)
    if cast:
        nwin_ref, nwout_ref = next(it), next(it)
    out_ref = next(it)
    if q_out:
        q_ref = next(it)
    if cast:
        cwin_ref, cwout_ref = next(it), next(it)
    xn_ref = next(it)

    f = pl.program_id(1)

    @pl.when(f == 0)
    def _():
        x = h_ref[...]
        if attn_in:
            x = x + jnp.dot(o_ref[...], wproj_ref[...], preferred_element_type=F32)
        xn_ref[...] = _rms(x, g_ref[...]).astype(BF16)
        out_ref[...] = x

    if cast:
        cwin_ref[...] = nwin_ref[...].astype(BF16)
        cwout_ref[...] = nwout_ref[...].astype(BF16)

    xn = xn_ref[...]
    part = None
    for c in range(FFN_TF // FFN_SUB):
        cols = slice(c * FFN_SUB, (c + 1) * FFN_SUB)
        gate = jnp.dot(xn, wg_ref[:, cols], preferred_element_type=F32)
        up = jnp.dot(xn, wu_ref[:, cols], preferred_element_type=F32)
        act = (gate * _sigmoid(gate) * up).astype(BF16)
        d = jnp.dot(act, wo_ref[cols, :], preferred_element_type=F32)
        part = d if part is None else part + d
    out_ref[...] += 0.5 * part

    @pl.when(f == n_f - 1)
    def _():
        hn = out_ref[...]
        if ple:
            xg = _rms(hn, pn_ref[...]).astype(BF16)
            gt = _sigmoid(jnp.dot(xg, pwg_ref[...], preferred_element_type=F32))
            pr = jnp.dot(p_ref[...].astype(BF16), pwp_ref[...], preferred_element_type=F32)
            hn = hn + gt * pr
        if final:
            hn = _rms(hn, fn_ref[...])
        out_ref[...] = hn
        if q_out:
            xq = _rms(hn, qg_ref[...]).astype(BF16)
            q_ref[...] = jnp.dot(xq, wq_ref[...], preferred_element_type=F32).astype(BF16)


def _ffn(h, layer, norm_g, w_in, w_out, attn_in=None, ple=None, final_g=None, q_out=None,
         cast_next=None):
    t = h.shape[0]
    tm, tf = FFN_TM, FFN_TF
    n_t, n_f = t // tm, D_FF // tf
    row = lambda i, f: (i, 0)
    fixed = lambda i, f: (0, 0)
    once = pl.Buffered(1)
    vec = pl.BlockSpec((1, D_MODEL), fixed, pipeline_mode=once)

    def square(lyr):
        return pl.BlockSpec((None, D_MODEL, D_MODEL), lambda i, f: (lyr, 0, 0), pipeline_mode=once)

    in_specs = [
        pl.BlockSpec((tm, D_MODEL), row),
        vec,
        pl.BlockSpec((None, D_MODEL, tf), lambda i, f: (layer, 0, f)),
        pl.BlockSpec((None, D_MODEL, tf), lambda i, f: (layer, 0, f + n_f)),
        pl.BlockSpec((None, tf, D_MODEL), lambda i, f: (layer, f, 0)),
    ]
    args = [h, norm_g.reshape(1, D_MODEL), w_in, w_in, w_out]
    if attn_in is not None:
        o, proj_layer, w_proj = attn_in
        in_specs += [pl.BlockSpec((tm, V_DIM), row), square(proj_layer)]
        args += [o, w_proj]
    if ple is not None:
        pn, ple_layer, pwg, p, pwp = ple
        in_specs += [
            vec,
            square(ple_layer),
            pl.BlockSpec((None, tm, PLE_DIM), lambda i, f: (ple_layer, i, 0)),
            pl.BlockSpec((None, PLE_DIM, D_MODEL), lambda i, f: (ple_layer, 0, 0),
                         pipeline_mode=once),
        ]
        args += [pn.reshape(1, D_MODEL), pwg, p, pwp]
    if final_g is not None:
        in_specs.append(vec)
        args.append(final_g.reshape(1, D_MODEL))
    if q_out is not None:
        g_q, q_layer, w_q = q_out
        in_specs += [vec, square(q_layer)]
        args += [g_q.reshape(1, D_MODEL), w_q]
    out_specs = [pl.BlockSpec((tm, D_MODEL), row)]
    out_shape = [jax.ShapeDtypeStruct((t, D_MODEL), F32)]
    if q_out is not None:
        out_specs.append(pl.BlockSpec((tm, QK_DIM), row))
        out_shape.append(jax.ShapeDtypeStruct((t, QK_DIM), BF16))
    if cast_next is not None:
        nw_in, nw_out, n_layer = cast_next
        steps = n_t * n_f
        r_in, r_out = D_MODEL // steps, D_FF // steps
        slab = lambda i, f: (i * n_f + f, 0)
        in_specs += [
            pl.BlockSpec((None, r_in, 2 * D_FF), lambda i, f: (n_layer, i * n_f + f, 0)),
            pl.BlockSpec((None, r_out, D_MODEL), lambda i, f: (n_layer, i * n_f + f, 0)),
        ]
        args += [nw_in, nw_out]
        out_specs += [pl.BlockSpec((r_in, 2 * D_FF), slab), pl.BlockSpec((r_out, D_MODEL), slab)]
        out_shape += [jax.ShapeDtypeStruct((D_MODEL, 2 * D_FF), BF16),
                      jax.ShapeDtypeStruct((D_FF, D_MODEL), BF16)]
    body = functools.partial(_ffn_body, n_f=n_f, attn_in=attn_in is not None, ple=ple is not None,
                             final=final_g is not None, q_out=q_out is not None,
                             cast=cast_next is not None)
    return pl.pallas_call(
        body,
        grid=(n_t, n_f),
        in_specs=in_specs,
        out_specs=out_specs,
        out_shape=out_shape,
        scratch_shapes=[pltpu.VMEM((tm, D_MODEL), BF16)],
        compiler_params=_params(("parallel", "arbitrary")),
        name="ffn",
    )(*args)


def _conv_body(h_ref, g_ref, win_ref, bin_ref, wdw_ref, bdw_ref, lng_ref, lnb_ref,
               wout_ref, bout_ref, out_ref, ubuf_ref, cbuf_ref, *, ts):
    n_lb = D_MODEL // LANES
    halo = CONV_HALO
    rows = CONV_ROWS
    first_tap_row = halo - (CONV_WIDTH - 1)

    @pl.when(pl.program_id(1) == 0)
    def _():
        ubuf_ref[:, 0:halo, :] = jnp.zeros((n_lb, halo, LANES), F32)

    x = h_ref[...]
    xn = _rms(x, g_ref[...]).astype(BF16)
    y = jnp.dot(xn, win_ref[...], preferred_element_type=F32) + bin_ref[...]
    u = y[:, :D_MODEL] * _sigmoid(y[:, D_MODEL:])
    for j in range(n_lb):
        ubuf_ref[j, halo:halo + ts, :] = u[:, j * LANES:(j + 1) * LANES]

    def lane_block(j, carry):
        for r in range(ts // rows):
            acc = jnp.zeros((rows // SUBLANES, SUBLANES, LANES), F32)
            for k in range(CONV_WIDTH):
                seg = ubuf_ref[j, pl.ds(r * rows + first_tap_row + k, rows), :]
                acc = acc + seg.reshape(rows // SUBLANES, SUBLANES, LANES) * wdw_ref[j, k][None]
            cbuf_ref[j, r * rows:(r + 1) * rows, :] = acc.reshape(rows, LANES)
        return carry

    lax.fori_loop(0, n_lb, lane_block, 0)

    ubuf_ref[:, 0:halo, :] = ubuf_ref[:, ts:ts + halo, :]

    c = jnp.concatenate([cbuf_ref[j] for j in range(n_lb)], axis=1) + bdw_ref[...]
    mu = jnp.mean(c, axis=-1, keepdims=True)
    d = c - mu
    var = jnp.mean(d * d, axis=-1, keepdims=True)
    z = d * lax.rsqrt(var + LN_EPS) * lng_ref[...] + lnb_ref[...]
    z = (z * _sigmoid(z)).astype(BF16)
    out_ref[...] = x + jnp.dot(z, wout_ref[...], preferred_element_type=F32) + bout_ref[...]


def _conv_layer(h, batch, seq, layer, norm_g, w_in, b_in, w_dw, b_dw, ln_g, ln_b, w_out, b_out):
    ts = CONV_TS
    n_s = seq // ts
    n_lb = D_MODEL // LANES
    wdw = jnp.broadcast_to(
        w_dw.reshape(CONV_WIDTH, n_lb, 1, LANES).transpose(1, 0, 2, 3),
        (n_lb, CONV_WIDTH, SUBLANES, LANES))
    row = lambda b, s: (b * n_s + s, 0)
    fixed = lambda b, s: (0, 0)
    vec = pl.BlockSpec((1, D_MODEL), fixed)
    return pl.pallas_call(
        functools.partial(_conv_body, ts=ts),
        grid=(batch, n_s),
        in_specs=[
            pl.BlockSpec((ts, D_MODEL), row),
            vec,
            pl.BlockSpec((None, D_MODEL, 2 * D_MODEL), lambda b, s: (layer, 0, 0)),
            pl.BlockSpec((1, 2 * D_MODEL), fixed),
            pl.BlockSpec((n_lb, CONV_WIDTH, SUBLANES, LANES), lambda b, s: (0, 0, 0, 0)),
            vec, vec, vec,
            pl.BlockSpec((None, D_MODEL, D_MODEL), lambda b, s: (layer, 0, 0)),
            vec,
        ],
        out_specs=pl.BlockSpec((ts, D_MODEL), row),
        out_shape=jax.ShapeDtypeStruct(h.shape, F32),
        scratch_shapes=[
            pltpu.VMEM((n_lb, CONV_HALO + ts, LANES), F32),
            pltpu.VMEM((n_lb, ts, LANES), F32),
        ],
        compiler_params=_params(("parallel", "arbitrary")),
        name="conformer_conv",
    )(h, norm_g.reshape(1, -1), w_in, b_in.reshape(1, -1), wdw, b_dw.reshape(1, -1),
      ln_g.reshape(1, -1), ln_b.reshape(1, -1), w_out, b_out.reshape(1, -1))


def _bucket_tiles(t):
    assert t >= MAX_DISTANCE
    qp = np.arange(t)[:, None]
    kp = np.arange(t)[None, :]
    max_exact = NUM_BUCKETS // 2

    def bucket(n):
        nf = np.maximum(n, 1).astype(np.float32)
        large = max_exact + (np.log(nf / max_exact) / math.log(MAX_DISTANCE / max_exact)
                             * (NUM_BUCKETS - max_exact)).astype(np.int32)
        return np.where(n < max_exact, n, np.minimum(large, NUM_BUCKETS - 1))

    diag = np.where(kp <= qp, bucket(np.maximum(qp - kp, 0)), -1)
    left = bucket(qp + t - kp)
    return np.stack([diag, left]).astype(np.int32)


def _bias_body(rb_ref, bucket_ref, out_ref):
    h = pl.program_id(0)
    far = rb_ref[NUM_BUCKETS - 1, h]
    bk = bucket_ref[...]
    acc = jnp.full(bk.shape, NEG_INF, F32)
    for b in range(NUM_BUCKETS):
        acc = jnp.where(bk == b, rb_ref[b, h] - far, acc)
    out_ref[...] = acc


def _rel_bias_tiles(rel_bias, t):
    return pl.pallas_call(
        _bias_body,
        grid=(N_HEADS,),
        in_specs=[
            pl.BlockSpec(memory_space=pltpu.SMEM),
            pl.BlockSpec((2, t, t), lambda h: (0, 0, 0)),
        ],
        out_specs=pl.BlockSpec((None, 2, t, t), lambda h: (h, 0, 0, 0)),
        out_shape=jax.ShapeDtypeStruct((N_HEADS, 2, t, t), F32),
        compiler_params=_params(("parallel",)),
        name="rel_bias_tiles",
    )(rel_bias.astype(F32), jnp.asarray(_bucket_tiles(t)))


def _attn_body(q_ref, kt_ref, v_ref, bias_ref, lam_ref, sub_ref, out_ref, vaug_ref,
               *, t, n_q, lambda_init):
    seq = v_ref.shape[0]
    vaug_ref[:, :LANES] = v_ref[...]
    vaug_ref[:, LANES:] = jnp.ones((seq, LANES), BF16)
    lq = lam_ref[...]
    lam = (jnp.exp(jnp.sum(lq[0:1] * lq[1:2], axis=-1, keepdims=True))
           - jnp.exp(jnp.sum(lq[2:3] * lq[3:4], axis=-1, keepdims=True)) + lambda_init)

    for i in reversed(range(n_q)):
        q = q_ref[i * t:(i + 1) * t, :]
        lane = lax.broadcasted_iota(jnp.int32, q.shape, 1)
        zero = jnp.zeros_like(q)
        q12 = jnp.concatenate([jnp.where(lane < HEAD_DIM, q, zero),
                               jnp.where(lane >= HEAD_DIM, q, zero)], axis=0)
        pieces = []
        if i >= 2:
            pieces.append((0, (i - 1) * t, None))
        if i >= 1:
            pieces.append(((i - 1) * t, i * t, 1))
        pieces.append((i * t, (i + 1) * t, 0))

        scores = []
        colmax = None
        for c0, c1, bias_idx in pieces:
            s = jnp.dot(q12, kt_ref[:, c0:c1], preferred_element_type=F32)
            if bias_idx is not None:
                s = (s.reshape(2, t, t) + bias_ref[bias_idx][None]).reshape(2 * t, t)
            for c in range((c1 - c0) // LANES):
                blk = s[:, c * LANES:(c + 1) * LANES]
                colmax = blk if colmax is None else jnp.maximum(colmax, blk)
            scores.append(s)
        m = jnp.max(colmax, axis=1, keepdims=True)

        acc = None
        for (c0, c1, _), s in zip(pieces, scores):
            p = jnp.exp(s - m).astype(BF16)
            d = jnp.dot(p, vaug_ref[c0:c1, :], preferred_element_type=F32)
            acc = d if acc is None else acc + d
        o12 = acc[:, :LANES] / acc[:, LANES:]
        o = o12[:t] - lam * o12[t:]
        out_ref[i * t:(i + 1) * t, :] = (
            _rms(o, sub_ref[...]) * (1.0 - lambda_init)).astype(out_ref.dtype)


def _attention(q, kt, v, bias_tiles, lam_rows, subln, batch, seq, lambda_init):
    t = ATT_T
    n_q = seq // t
    q3 = q.reshape(batch, seq, QK_DIM)
    head_cols = pl.BlockSpec((None, seq, LANES), lambda b, h: (b, 0, h))
    out = pl.pallas_call(
        functools.partial(_attn_body, t=t, n_q=n_q, lambda_init=lambda_init),
        grid=(batch, N_HEADS),
        in_specs=[
            head_cols,
            pl.BlockSpec((None, LANES, seq), lambda b, h: (b, h, 0)),
            head_cols,
            pl.BlockSpec((None, 2, t, t), lambda b, h: (h, 0, 0, 0)),
            pl.BlockSpec((4, HEAD_DIM), lambda b, h: (0, 0)),
            pl.BlockSpec((1, LANES), lambda b, h: (0, 0)),
        ],
        out_specs=head_cols,
        out_shape=jax.ShapeDtypeStruct((batch, seq, V_DIM), BF16),
        scratch_shapes=[pltpu.VMEM((seq, 2 * LANES), BF16)],
        compiler_params=_params(("parallel", "parallel")),
        name="diff_attention",
    )(q3, kt, v, bias_tiles, lam_rows, subln.reshape(1, LANES))
    return out.reshape(batch * seq, V_DIM)


def _kv_body(h_ref, g_ref, wkt_ref, wv_ref, kt_ref, v_ref):
    xn = _rms(h_ref[...], g_ref[...]).astype(BF16)
    kt = lax.dot_general(wkt_ref[...], xn, (((1,), (1,)), ((), ())),
                         preferred_element_type=F32)
    kt_ref[...] = kt.astype(BF16)
    v_ref[...] = jnp.dot(xn, wv_ref[...], preferred_element_type=F32).astype(BF16)


def _shared_kv(h, batch, seq, norm_g, w_kt, w_v):
    tm = PROJ_TM
    n_s = seq // tm
    fixed = lambda b, s: (0, 0)
    return pl.pallas_call(
        _kv_body,
        grid=(batch, n_s),
        in_specs=[
            pl.BlockSpec((tm, D_MODEL), lambda b, s: (b * n_s + s, 0)),
            pl.BlockSpec((1, D_MODEL), fixed),
            pl.BlockSpec((QK_DIM, D_MODEL), fixed),
            pl.BlockSpec((D_MODEL, V_DIM), fixed),
        ],
        out_specs=[
            pl.BlockSpec((None, QK_DIM, tm), lambda b, s: (b, 0, s)),
            pl.BlockSpec((None, tm, V_DIM), lambda b, s: (b, s, 0)),
        ],
        out_shape=[
            jax.ShapeDtypeStruct((batch, QK_DIM, seq), BF16),
            jax.ShapeDtypeStruct((batch, seq, V_DIM), BF16),
        ],
        compiler_params=_params(("parallel", "parallel")),
        name="shared_kv",
    )(h, norm_g.reshape(1, D_MODEL), w_kt, w_v)


def _head_major(w):
    lead = w.shape[:-1]
    w = w.reshape(*lead, 2, N_HEADS, HEAD_DIM)
    return jnp.swapaxes(w, -3, -2).reshape(*lead, QK_DIM)


def kernel(x, p, ffn1_norm, ffn1_w_in, ffn1_w_out, mix_norm, ffn2_norm, ffn2_w_in, ffn2_w_out,
           ple_norm, ple_w_gate, ple_w_proj, conv_w_in, conv_b_in, conv_w_dw, conv_b_dw,
           conv_ln_g, conv_ln_b, conv_w_out, conv_b_out, kv_norm, w_kv, attn_w_q,
           attn_lq1, attn_lk1, attn_lq2, attn_lk2, attn_subln, attn_w_o, rel_bias, final_norm):
    batch, seq, _ = x.shape
    tokens = batch * seq
    h = x.reshape(tokens, D_MODEL)
    p2 = p.reshape(DEPTH, tokens, PLE_DIM)
    bias_tiles = _rel_bias_tiles(rel_bias, ATT_T)
    w_kt = _head_major(w_kv[:, :QK_DIM]).T.astype(BF16)
    w_v = w_kv[:, QK_DIM:].astype(BF16)
    ple_w_gate, ple_w_proj = ple_w_gate.astype(BF16), ple_w_proj.astype(BF16)
    conv_w_in, conv_w_out = conv_w_in.astype(BF16), conv_w_out.astype(BF16)
    attn_w_o = attn_w_o.astype(BF16)
    w_q = (_head_major(attn_w_q) * (HEAD_DIM ** -0.5)).astype(BF16)
    w_in, w_out = ffn1_w_in[0:1].astype(BF16), ffn1_w_out[0:1].astype(BF16)
    kt = v = o = None
    for i in range(DEPTH):
        j = i - N_A
        if i == N_A:
            kt, v = _shared_kv(h, batch, seq, kv_norm, w_kt, w_v)
        res = _ffn(h, 0, ffn1_norm[i], w_in, w_out,
                   q_out=(mix_norm[i], j, w_q) if i >= N_A else None,
                   cast_next=(ffn2_w_in, ffn2_w_out, i))
        h, w_in, w_out = res[0], res[-2][None], res[-1][None]
        if i < N_A:
            h = _conv_layer(h, batch, seq, i, mix_norm[i], conv_w_in, conv_b_in[i],
                            conv_w_dw[i], conv_b_dw[i], conv_ln_g[i], conv_ln_b[i],
                            conv_w_out, conv_b_out[i])
        else:
            lambda_init = 0.8 - 0.6 * math.exp(-0.3 * i)
            lam_rows = jnp.stack([attn_lq1[j], attn_lk1[j], attn_lq2[j], attn_lk2[j]]).astype(F32)
            o = _attention(res[1], kt, v, bias_tiles, lam_rows, attn_subln[j], batch, seq,
                           lambda_init)
        last = i == DEPTH - 1
        res = _ffn(h, 0, ffn2_norm[i], w_in, w_out,
                   attn_in=(o, j, attn_w_o) if i >= N_A else None,
                   ple=(ple_norm[i], i, ple_w_gate, p2, ple_w_proj),
                   final_g=final_norm if last else None,
                   cast_next=None if last else (ffn1_w_in, ffn1_w_out, i + 1))
        h = res[0]
        if not last:
            w_in, w_out = res[-2][None], res[-1][None]
    return h.reshape(batch, seq, D_MODEL)
```

```python
import functools
import math

import jax
import jax.numpy as jnp
import numpy as np
from jax import lax
from jax.experimental import pallas as pl
from jax.experimental.pallas import tpu as pltpu

D_MODEL = 1024
DEPTH = 4
N_A = DEPTH // 2
D_FF = 4 * D_MODEL
CONV_WIDTH = 31
HEAD_DIM = 64
N_HEADS = D_MODEL // (2 * HEAD_DIM)
QK_DIM = 2 * N_HEADS * HEAD_DIM
V_DIM = N_HEADS * 2 * HEAD_DIM
NUM_BUCKETS = 32
MAX_DISTANCE = 128
PLE_DIM = 256
RMS_EPS = 1e-6
LN_EPS = 1e-5
NEG_INF = -1e30

LANES = 128
SUBLANES = 8
VMEM_LIMIT_BYTES = 56 * 1024 * 1024

FFN_TM = 1024
FFN_TF = 1024
FFN_SUB = 512
PROJ_TM = 1024
CONV_TS = 512
CONV_HALO = 32
CONV_ROWS = 32
ATT_T = 256
ATT_HEADS = 2

F32 = jnp.float32
BF16 = jnp.bfloat16


def _rms(x, g):
    ms = jnp.mean(x * x, axis=-1, keepdims=True)
    return x * lax.rsqrt(ms + RMS_EPS) * g


def _sigmoid(x):
    return 1.0 / (1.0 + jnp.exp(-x))


def _params(sem):
    return pltpu.CompilerParams(dimension_semantics=sem, vmem_limit_bytes=VMEM_LIMIT_BYTES)


def _ffn_body(*refs, n_f, attn_in, ple, final, q_out, cast):
    it = iter(refs)
    h_ref, g_ref, wg_ref, wu_ref, wo_ref = (next(it) for _ in range(5))
    if attn_in:
        o_ref, wproj_ref = next(it), next(it)
    if ple:
        pn_ref, pwg_ref, p_ref, pwp_ref = (next(it) for _ in range(4))
    if final:
        fn_ref = next(it)
    if q_out:
        qg_ref, wq_ref = next(it), next(it)
    if cast:
        nwin_ref, nwout_ref = next(it), next(it)
    out_ref = next(it)
    if q_out:
        q_ref = next(it)
    if cast:
        cwin_ref, cwout_ref = next(it), next(it)
    xn_ref = next(it)

    f = pl.program_id(1)

    @pl.when(f == 0)
    def _():
        x = h_ref[...]
        if attn_in:
            x = x + jnp.dot(o_ref[...], wproj_ref[...], preferred_element_type=F32)
        xn_ref[...] = _rms(x, g_ref[...]).astype(BF16)
        out_ref[...] = x

    if cast:
        cwin_ref[...] = nwin_ref[...].astype(BF16)
        cwout_ref[...] = nwout_ref[...].astype(BF16)

    xn = xn_ref[...]
    part = None
    for c in range(FFN_TF // FFN_SUB):
        cols = slice(c * FFN_SUB, (c + 1) * FFN_SUB)
        gate = jnp.dot(xn, wg_ref[:, cols], preferred_element_type=F32)
        up = jnp.dot(xn, wu_ref[:, cols], preferred_element_type=F32)
        act = (gate * _sigmoid(gate) * up).astype(BF16)
        d = jnp.dot(act, wo_ref[cols, :], preferred_element_type=F32)
        part = d if part is None else part + d
    out_ref[...] += 0.5 * part

    @pl.when(f == n_f - 1)
    def _():
        hn = out_ref[...]
        if ple:
            xg = _rms(hn, pn_ref[...]).astype(BF16)
            gt = _sigmoid(jnp.dot(xg, pwg_ref[...], preferred_element_type=F32))
            pr = jnp.dot(p_ref[...].astype(BF16), pwp_ref[...], preferred_element_type=F32)
            hn = hn + gt * pr
        if final:
            hn = _rms(hn, fn_ref[...])
        out_ref[...] = hn
        if q_out:
            xq = _rms(hn, qg_ref[...]).astype(BF16)
            q_ref[...] = jnp.dot(xq, wq_ref[...], preferred_element_type=F32).astype(BF16)


def _ffn(h, layer, norm_g, w_in, w_out, attn_in=None, ple=None, final_g=None, q_out=None,
         cast_next=None):
    t = h.shape[0]
    tm, tf = FFN_TM, FFN_TF
    n_t, n_f = t // tm, D_FF // tf
    row = lambda i, f: (i, 0)
    fixed = lambda i, f: (0, 0)
    once = pl.Buffered(1)
    vec = pl.BlockSpec((1, D_MODEL), fixed, pipeline_mode=once)

    def square(lyr):
        return pl.BlockSpec((None, D_MODEL, D_MODEL), lambda i, f: (lyr, 0, 0), pipeline_mode=once)

    in_specs = [
        pl.BlockSpec((tm, D_MODEL), row),
        vec,
        pl.BlockSpec((None, D_MODEL, tf), lambda i, f: (layer, 0, f)),
        pl.BlockSpec((None, D_MODEL, tf), lambda i, f: (layer, 0, f + n_f)),
        pl.BlockSpec((None, tf, D_MODEL), lambda i, f: (layer, f, 0)),
    ]
    args = [h, norm_g.reshape(1, D_MODEL), w_in, w_in, w_out]
    if attn_in is not None:
        o, proj_layer, w_proj = attn_in
        in_specs += [pl.BlockSpec((tm, V_DIM), row), square(proj_layer)]
        args += [o, w_proj]
    if ple is not None:
        pn, ple_layer, pwg, p, pwp = ple
        in_specs += [
            vec,
            square(ple_layer),
            pl.BlockSpec((None, tm, PLE_DIM), lambda i, f: (ple_layer, i, 0)),
            pl.BlockSpec((None, PLE_DIM, D_MODEL), lambda i, f: (ple_layer, 0, 0),
                         pipeline_mode=once),
        ]
        args += [pn.reshape(1, D_MODEL), pwg, p, pwp]
    if final_g is not None:
        in_specs.append(vec)
        args.append(final_g.reshape(1, D_MODEL))
    if q_out is not None:
        g_q, q_layer, w_q = q_out
        in_specs += [vec, square(q_layer)]
        args += [g_q.reshape(1, D_MODEL), w_q]
    out_specs = [pl.BlockSpec((tm, D_MODEL), row)]
    out_shape = [jax.ShapeDtypeStruct((t, D_MODEL), F32)]
    if q_out is not None:
        out_specs.append(pl.BlockSpec((tm, QK_DIM), row))
        out_shape.append(jax.ShapeDtypeStruct((t, QK_DIM), BF16))
    if cast_next is not None:
        nw_in, nw_out, n_layer = cast_next
        steps = n_t * n_f
        r_in, r_out = D_MODEL // steps, D_FF // steps
        slab = lambda i, f: (i * n_f + f, 0)
        in_specs += [
            pl.BlockSpec((None, r_in, 2 * D_FF), lambda i, f: (n_layer, i * n_f + f, 0)),
            pl.BlockSpec((None, r_out, D_MODEL), lambda i, f: (n_layer, i * n_f + f, 0)),
        ]
        args += [nw_in, nw_out]
        out_specs += [pl.BlockSpec((r_in, 2 * D_FF), slab), pl.BlockSpec((r_out, D_MODEL), slab)]
        out_shape += [jax.ShapeDtypeStruct((D_MODEL, 2 * D_FF), BF16),
                      jax.ShapeDtypeStruct((D_FF, D_MODEL), BF16)]
    body = functools.partial(_ffn_body, n_f=n_f, attn_in=attn_in is not None, ple=ple is not None,
                             final=final_g is not None, q_out=q_out is not None,
                             cast=cast_next is not None)
    return pl.pallas_call(
        body,
        grid=(n_t, n_f),
        in_specs=in_specs,
        out_specs=out_specs,
        out_shape=out_shape,
        scratch_shapes=[pltpu.VMEM((tm, D_MODEL), BF16)],
        compiler_params=_params(("parallel", "arbitrary")),
        name="ffn",
    )(*args)


def _conv_body(h_ref, g_ref, win_ref, bin_ref, wdw_ref, bdw_ref, lng_ref, lnb_ref,
               wout_ref, bout_ref, out_ref, ubuf_ref, cbuf_ref, *, ts):
    n_lb = D_MODEL // LANES
    halo = CONV_HALO
    rows = CONV_ROWS
    first_tap_row = halo - (CONV_WIDTH - 1)

    @pl.when(pl.program_id(1) == 0)
    def _():
        ubuf_ref[:, 0:halo, :] = jnp.zeros((n_lb, halo, LANES), F32)

    x = h_ref[...]
    xn = _rms(x, g_ref[...]).astype(BF16)
    y = jnp.dot(xn, win_ref[...], preferred_element_type=F32) + bin_ref[...]
    u = y[:, :D_MODEL] * _sigmoid(y[:, D_MODEL:])
    for j in range(n_lb):
        ubuf_ref[j, halo:halo + ts, :] = u[:, j * LANES:(j + 1) * LANES]

    def lane_block(j, carry):
        for r in range(ts // rows):
            acc = jnp.zeros((rows // SUBLANES, SUBLANES, LANES), F32)
            for k in range(CONV_WIDTH):
                seg = ubuf_ref[j, pl.ds(r * rows + first_tap_row + k, rows), :]
                acc = acc + seg.reshape(rows // SUBLANES, SUBLANES, LANES) * wdw_ref[j, k][None]
            cbuf_ref[j, r * rows:(r + 1) * rows, :] = acc.reshape(rows, LANES)
        return carry

    lax.fori_loop(0, n_lb, lane_block, 0)

    ubuf_ref[:, 0:halo, :] = ubuf_ref[:, ts:ts + halo, :]

    c = jnp.concatenate([cbuf_ref[j] for j in range(n_lb)], axis=1) + bdw_ref[...]
    mu = jnp.mean(c, axis=-1, keepdims=True)
    d = c - mu
    var = jnp.mean(d * d, axis=-1, keepdims=True)
    z = d * lax.rsqrt(var + LN_EPS) * lng_ref[...] + lnb_ref[...]
    z = (z * _sigmoid(z)).astype(BF16)
    out_ref[...] = x + jnp.dot(z, wout_ref[...], preferred_element_type=F32) + bout_ref[...]


def _conv_layer(h, batch, seq, layer, norm_g, w_in, b_in, w_dw, b_dw, ln_g, ln_b, w_out, b_out):
    ts = CONV_TS
    n_s = seq // ts
    n_lb = D_MODEL // LANES
    wdw = jnp.broadcast_to(
        w_dw.reshape(CONV_WIDTH, n_lb, 1, LANES).transpose(1, 0, 2, 3),
        (n_lb, CONV_WIDTH, SUBLANES, LANES))
    row = lambda b, s: (b * n_s + s, 0)
    fixed = lambda b, s: (0, 0)
    vec = pl.BlockSpec((1, D_MODEL), fixed)
    return pl.pallas_call(
        functools.partial(_conv_body, ts=ts),
        grid=(batch, n_s),
        in_specs=[
            pl.BlockSpec((ts, D_MODEL), row),
            vec,
            pl.BlockSpec((None, D_MODEL, 2 * D_MODEL), lambda b, s: (layer, 0, 0)),
            pl.BlockSpec((1, 2 * D_MODEL), fixed),
            pl.BlockSpec((n_lb, CONV_WIDTH, SUBLANES, LANES), lambda b, s: (0, 0, 0, 0)),
            vec, vec, vec,
            pl.BlockSpec((None, D_MODEL, D_MODEL), lambda b, s: (layer, 0, 0)),
            vec,
        ],
        out_specs=pl.BlockSpec((ts, D_MODEL), row),
        out_shape=jax.ShapeDtypeStruct(h.shape, F32),
        scratch_shapes=[
            pltpu.VMEM((n_lb, CONV_HALO + ts, LANES), F32),
            pltpu.VMEM((n_lb, ts, LANES), F32),
        ],
        compiler_params=_params(("parallel", "arbitrary")),
        name="conformer_conv",
    )(h, norm_g.reshape(1, -1), w_in, b_in.reshape(1, -1), wdw, b_dw.reshape(1, -1),
      ln_g.reshape(1, -1), ln_b.reshape(1, -1), w_out, b_out.reshape(1, -1))


def _bucket_tiles(t):
    assert t >= MAX_DISTANCE
    qp = np.arange(t)[:, None]
    kp = np.arange(t)[None, :]
    max_exact = NUM_BUCKETS // 2

    def bucket(n):
        nf = np.maximum(n, 1).astype(np.float32)
        large = max_exact + (np.log(nf / max_exact) / math.log(MAX_DISTANCE / max_exact)
                             * (NUM_BUCKETS - max_exact)).astype(np.int32)
        return np.where(n < max_exact, n, np.minimum(large, NUM_BUCKETS - 1))

    diag = np.where(kp <= qp, bucket(np.maximum(qp - kp, 0)), -1)
    left = bucket(qp + t - kp)
    return np.stack([diag, left]).astype(np.int32)


def _bias_body(rb_ref, bucket_ref, out_ref):
    h = pl.program_id(0)
    far = rb_ref[NUM_BUCKETS - 1, h]
    bk = bucket_ref[...]
    acc = jnp.full(bk.shape, NEG_INF, F32)
    for b in range(NUM_BUCKETS):
        acc = jnp.where(bk == b, rb_ref[b, h] - far, acc)
    out_ref[...] = acc


def _rel_bias_tiles(rel_bias, t):
    return pl.pallas_call(
        _bias_body,
        grid=(N_HEADS,),
        in_specs=[
            pl.BlockSpec(memory_space=pltpu.SMEM),
            pl.BlockSpec((2, t, t), lambda h: (0, 0, 0)),
        ],
        out_specs=pl.BlockSpec((None, 2, t, t), lambda h: (h, 0, 0, 0)),
        out_shape=jax.ShapeDtypeStruct((N_HEADS, 2, t, t), F32),
        compiler_params=_params(("parallel",)),
        name="rel_bias_tiles",
    )(rel_bias.astype(F32), jnp.asarray(_bucket_tiles(t)))


def _attn_body(q_ref, kt_ref, v_ref, bias_ref, lam_ref, sub_ref, out_ref, vaug_ref,
               *, t, n_q, lambda_init):
    seq = v_ref.shape[0]
    lq = lam_ref[...]
    lam = (jnp.exp(jnp.sum(lq[0:1] * lq[1:2], axis=-1, keepdims=True))
           - jnp.exp(jnp.sum(lq[2:3] * lq[3:4], axis=-1, keepdims=True)) + lambda_init)
    for hh in range(ATT_HEADS):
        vaug_ref[hh, :, :LANES] = v_ref[:, hh * LANES:(hh + 1) * LANES]
        vaug_ref[hh, :, LANES:] = jnp.ones((seq, LANES), BF16)

    for hh, i in [(hh, i) for hh in range(ATT_HEADS) for i in reversed(range(n_q))]:
        head = slice(hh * LANES, (hh + 1) * LANES)
        q = q_ref[i * t:(i + 1) * t, head]
        lane = lax.broadcasted_iota(jnp.int32, q.shape, 1)
        zero = jnp.zeros_like(q)
        q12 = jnp.concatenate([jnp.where(lane < HEAD_DIM, q, zero),
                               jnp.where(lane >= HEAD_DIM, q, zero)], axis=0)
        pieces = []
        if i >= 2:
            pieces.append((0, (i - 1) * t, None))
        if i >= 1:
            pieces.append(((i - 1) * t, i * t, 1))
        pieces.append((i * t, (i + 1) * t, 0))

        scores = []
        colmax = None
        for c0, c1, bias_idx in pieces:
            s = jnp.dot(q12, kt_ref[head, c0:c1], preferred_element_type=F32)
            if bias_idx is not None:
                s = (s.reshape(2, t, t) + bias_ref[hh, bias_idx][None]).reshape(2 * t, t)
            for c in range((c1 - c0) // LANES):
                blk = s[:, c * LANES:(c + 1) * LANES]
                colmax = blk if colmax is None else jnp.maximum(colmax, blk)
            scores.append(s)
        m = jnp.max(colmax, axis=1, keepdims=True)

        acc = None
        for (c0, c1, _), s in zip(pieces, scores):
            p = jnp.exp(s - m).astype(BF16)
            d = jnp.dot(p, vaug_ref[hh, c0:c1, :], preferred_element_type=F32)
            acc = d if acc is None else acc + d
        o12 = acc[:, :LANES] / acc[:, LANES:]
        o = o12[:t] - lam * o12[t:]
        out_ref[i * t:(i + 1) * t, head] = (
            _rms(o, sub_ref[...]) * (1.0 - lambda_init)).astype(out_ref.dtype)


def _attention(q, kt, v, bias_tiles, lam_rows, subln, batch, seq, lambda_init):
    t = ATT_T
    n_q = seq // t
    q3 = q.reshape(batch, seq, QK_DIM)
    width = ATT_HEADS * LANES
    head_cols = pl.BlockSpec((None, seq, width), lambda b, h: (b, 0, h))
    out = pl.pallas_call(
        functools.partial(_attn_body, t=t, n_q=n_q, lambda_init=lambda_init),
        grid=(batch, N_HEADS // ATT_HEADS),
        in_specs=[
            head_cols,
            pl.BlockSpec((None, width, seq), lambda b, h: (b, h, 0)),
            head_cols,
            pl.BlockSpec((ATT_HEADS, 2, t, t), lambda b, h: (h, 0, 0, 0)),
            pl.BlockSpec((4, HEAD_DIM), lambda b, h: (0, 0)),
            pl.BlockSpec((1, LANES), lambda b, h: (0, 0)),
        ],
        out_specs=head_cols,
        out_shape=jax.ShapeDtypeStruct((batch, seq, V_DIM), BF16),
        scratch_shapes=[pltpu.VMEM((ATT_HEADS, seq, 2 * LANES), BF16)],
        compiler_params=_params(("parallel", "parallel")),
        name="diff_attention",
    )(q3, kt, v, bias_tiles, lam_rows, subln.reshape(1, LANES))
    return out.reshape(batch * seq, V_DIM)


def _kv_body(h_ref, g_ref, wkt_ref, wv_ref, kt_ref, v_ref):
    xn = _rms(h_ref[...], g_ref[...]).astype(BF16)
    kt = lax.dot_general(wkt_ref[...], xn, (((1,), (1,)), ((), ())),
                         preferred_element_type=F32)
    kt_ref[...] = kt.astype(BF16)
    v_ref[...] = jnp.dot(xn, wv_ref[...], preferred_element_type=F32).astype(BF16)


def _shared_kv(h, batch, seq, norm_g, w_kt, w_v):
    tm = PROJ_TM
    n_s = seq // tm
    fixed = lambda b, s: (0, 0)
    return pl.pallas_call(
        _kv_body,
        grid=(batch, n_s),
        in_specs=[
            pl.BlockSpec((tm, D_MODEL), lambda b, s: (b * n_s + s, 0)),
            pl.BlockSpec((1, D_MODEL), fixed),
            pl.BlockSpec((QK_DIM, D_MODEL), fixed),
            pl.BlockSpec((D_MODEL, V_DIM), fixed),
        ],
        out_specs=[
            pl.BlockSpec((None, QK_DIM, tm), lambda b, s: (b, 0, s)),
            pl.BlockSpec((None, tm, V_DIM), lambda b, s: (b, s, 0)),
        ],
        out_shape=[
            jax.ShapeDtypeStruct((batch, QK_DIM, seq), BF16),
            jax.ShapeDtypeStruct((batch, seq, V_DIM), BF16),
        ],
        compiler_params=_params(("parallel", "parallel")),
        name="shared_kv",
    )(h, norm_g.reshape(1, D_MODEL), w_kt, w_v)


def _head_major(w):
    lead = w.shape[:-1]
    w = w.reshape(*lead, 2, N_HEADS, HEAD_DIM)
    return jnp.swapaxes(w, -3, -2).reshape(*lead, QK_DIM)


def kernel(x, p, ffn1_norm, ffn1_w_in, ffn1_w_out, mix_norm, ffn2_norm, ffn2_w_in, ffn2_w_out,
           ple_norm, ple_w_gate, ple_w_proj, conv_w_in, conv_b_in, conv_w_dw, conv_b_dw,
           conv_ln_g, conv_ln_b, conv_w_out, conv_b_out, kv_norm, w_kv, attn_w_q,
           attn_lq1, attn_lk1, attn_lq2, attn_lk2, attn_subln, attn_w_o, rel_bias, final_norm):
    batch, seq, _ = x.shape
    tokens = batch * seq
    h = x.reshape(tokens, D_MODEL)
    p2 = p.reshape(DEPTH, tokens, PLE_DIM)
    bias_tiles = _rel_bias_tiles(rel_bias, ATT_T)
    w_kt = _head_major(w_kv[:, :QK_DIM]).T.astype(BF16)
    w_v = w_kv[:, QK_DIM:].astype(BF16)
    ple_w_gate, ple_w_proj = ple_w_gate.astype(BF16), ple_w_proj.astype(BF16)
    conv_w_in, conv_w_out = conv_w_in.astype(BF16), conv_w_out.astype(BF16)
    attn_w_o = attn_w_o.astype(BF16)
    w_q = (_head_major(attn_w_q) * (HEAD_DIM ** -0.5)).astype(BF16)
    w_in, w_out = ffn1_w_in[0:1].astype(BF16), ffn1_w_out[0:1].astype(BF16)
    kt = v = o = None
    for i in range(DEPTH):
        j = i - N_A
        if i == N_A:
            kt, v = _shared_kv(h, batch, seq, kv_norm, w_kt, w_v)
        res = _ffn(h, 0, ffn1_norm[i], w_in, w_out,
                   q_out=(mix_norm[i], j, w_q) if i >= N_A else None,
                   cast_next=(ffn2_w_in, ffn2_w_out, i))
        h, w_in, w_out = res[0], res[-2][None], res[-1][None]
        if i < N_A:
            h = _conv_layer(h, batch, seq, i, mix_norm[i], conv_w_in, conv_b_in[i],
                            conv_w_dw[i], conv_b_dw[i], conv_ln_g[i], conv_ln_b[i],
                            conv_w_out, conv_b_out[i])
        else:
            lambda_init = 0.8 - 0.6 * math.exp(-0.3 * i)
            lam_rows = jnp.stack([attn_lq1[j], attn_lk1[j], attn_lq2[j], attn_lk2[j]]).astype(F32)
            o = _attention(res[1], kt, v, bias_tiles, lam_rows, attn_subln[j], batch, seq,
                           lambda_init)
        last = i == DEPTH - 1
        res = _ffn(h, 0, ffn2_norm[i], w_in, w_out,
                   attn_in=(o, j, attn_w_o) if i >= N_A else None,
                   ple=(ple_norm[i], i, ple_w_gate, p2, ple_w_proj),
                   final_g=final_norm if last else None,
                   cast_next=None if last else (ffn1_w_in, ffn1_w_out, i + 1))
        h = res[0]
        if not last:
            w_in, w_out = res[-2][None], res[-1][None]
    return h.reshape(batch, seq, D_MODEL)
```

```python
import functools
import math

import jax
import jax.numpy as jnp
import numpy as np
from jax import lax
from jax.experimental import pallas as pl
from jax.experimental.pallas import tpu as pltpu

D_MODEL = 1024
DEPTH = 4
N_A = DEPTH // 2
D_FF = 4 * D_MODEL
CONV_WIDTH = 31
HEAD_DIM = 64
N_HEADS = D_MODEL // (2 * HEAD_DIM)
QK_DIM = 2 * N_HEADS * HEAD_DIM
V_DIM = N_HEADS * 2 * HEAD_DIM
NUM_BUCKETS = 32
MAX_DISTANCE = 128
PLE_DIM = 256
RMS_EPS = 1e-6
LN_EPS = 1e-5
NEG_INF = -1e30

LANES = 128
SUBLANES = 8
PACK = 16
VMEM_LIMIT_BYTES = 56 * 1024 * 1024

FFN_TM = 1024
FFN_TF = 1024
FFN_SUB = 512
PROJ_TM = 1024
CONV_TS = 512
CONV_HALO = 32
ATT_T = 256
ATT_HEADS = 2

F32 = jnp.float32
BF16 = jnp.bfloat16


def _rms(x, g):
    ms = jnp.mean(x * x, axis=-1, keepdims=True)
    return x * lax.rsqrt(ms + RMS_EPS) * g


def _sigmoid(x):
    return 1.0 / (1.0 + jnp.exp(-x))


def _params(sem):
    return pltpu.CompilerParams(dimension_semantics=sem, vmem_limit_bytes=VMEM_LIMIT_BYTES)


def _ffn_body(*refs, n_f, attn_in, ple, final, q_out, cast):
    it = iter(refs)
    h_ref, g_ref, wg_ref, wu_ref, wo_ref = (next(it) for _ in range(5))
    if attn_in:
        o_ref, wproj_ref = next(it), next(it)
    if ple:
        pn_ref, pwg_ref, p_ref, pwp_ref = (next(it) for _ in range(4))
    if final:
        fn_ref = next(it)
    if q_out:
        qg_ref, wq_ref = next(it), next(it)
    if cast:
        nwin_ref, nwout_ref = next(it), next(it)
    out_ref = next(it)
    if q_out:
        q_ref = next(it)
    if cast:
        cwin_ref, cwout_ref = next(it), next(it)
    xn_ref = next(it)

    f = pl.program_id(1)

    @pl.when(f == 0)
    def _():
        x = h_ref[...]
        if attn_in:
            x = x + jnp.dot(o_ref[...], wproj_ref[...], preferred_element_type=F32)
        xn_ref[...] = _rms(x, g_ref[...]).astype(BF16)
        out_ref[...] = x

    if cast:
        cwin_ref[...] = nwin_ref[...].astype(BF16)
        cwout_ref[...] = nwout_ref[...].astype(BF16)

    xn = xn_ref[...]
    part = None
    for c in range(FFN_TF // FFN_SUB):
        cols = slice(c * FFN_SUB, (c + 1) * FFN_SUB)
        gate = jnp.dot(xn, wg_ref[:, cols], preferred_element_type=F32)
        up = jnp.dot(xn, wu_ref[:, cols], preferred_element_type=F32)
        act = (gate * _sigmoid(gate) * up).astype(BF16)
        d = jnp.dot(act, wo_ref[cols, :], preferred_element_type=F32)
        part = d if part is None else part + d
    out_ref[...] += 0.5 * part

    @pl.when(f == n_f - 1)
    def _():
        hn = out_ref[...]
        if ple:
            xg = _rms(hn, pn_ref[...]).astype(BF16)
            gt = _sigmoid(jnp.dot(xg, pwg_ref[...], preferred_element_type=F32))
            pr = jnp.dot(p_ref[...].astype(BF16), pwp_ref[...], preferred_element_type=F32)
            hn = hn + gt * pr
        if final:
            hn = _rms(hn, fn_ref[...])
        out_ref[...] = hn
        if q_out:
            xq = _rms(hn, qg_ref[...]).astype(BF16)
            q_ref[...] = jnp.dot(xq, wq_ref[...], preferred_element_type=F32).astype(BF16)


def _ffn(h, layer, norm_g, w_in, w_out, attn_in=None, ple=None, final_g=None, q_out=None,
         cast_next=None):
    t = h.shape[0]
    tm, tf = FFN_TM, FFN_TF
    n_t, n_f = t // tm, D_FF // tf
    row = lambda i, f: (i, 0)
    fixed = lambda i, f: (0, 0)
    once = pl.Buffered(1)
    vec = pl.BlockSpec((1, D_MODEL), fixed, pipeline_mode=once)

    def square(lyr):
        return pl.BlockSpec((None, D_MODEL, D_MODEL), lambda i, f: (lyr, 0, 0), pipeline_mode=once)

    in_specs = [
        pl.BlockSpec((tm, D_MODEL), row),
        vec,
        pl.BlockSpec((None, D_MODEL, tf), lambda i, f: (layer, 0, f)),
        pl.BlockSpec((None, D_MODEL, tf), lambda i, f: (layer, 0, f + n_f)),
        pl.BlockSpec((None, tf, D_MODEL), lambda i, f: (layer, f, 0)),
    ]
    args = [h, norm_g.reshape(1, D_MODEL), w_in, w_in, w_out]
    if attn_in is not None:
        o, proj_layer, w_proj = attn_in
        in_specs += [pl.BlockSpec((tm, V_DIM), row), square(proj_layer)]
        args += [o, w_proj]
    if ple is not None:
        pn, ple_layer, pwg, p, pwp = ple
        in_specs += [
            vec,
            square(ple_layer),
            pl.BlockSpec((None, tm, PLE_DIM), lambda i, f: (ple_layer, i, 0)),
            pl.BlockSpec((None, PLE_DIM, D_MODEL), lambda i, f: (ple_layer, 0, 0),
                         pipeline_mode=once),
        ]
        args += [pn.reshape(1, D_MODEL), pwg, p, pwp]
    if final_g is not None:
        in_specs.append(vec)
        args.append(final_g.reshape(1, D_MODEL))
    if q_out is not None:
        g_q, q_layer, w_q = q_out
        in_specs += [vec, square(q_layer)]
        args += [g_q.reshape(1, D_MODEL), w_q]
    out_specs = [pl.BlockSpec((tm, D_MODEL), row)]
    out_shape = [jax.ShapeDtypeStruct((t, D_MODEL), F32)]
    if q_out is not None:
        out_specs.append(pl.BlockSpec((tm, QK_DIM), row))
        out_shape.append(jax.ShapeDtypeStruct((t, QK_DIM), BF16))
    if cast_next is not None:
        nw_in, nw_out, n_layer = cast_next
        steps = n_t * n_f
        r_in, r_out = D_MODEL // steps, D_FF // steps
        slab = lambda i, f: (i * n_f + f, 0)
        in_specs += [
            pl.BlockSpec((None, r_in, 2 * D_FF), lambda i, f: (n_layer, i * n_f + f, 0)),
            pl.BlockSpec((None, r_out, D_MODEL), lambda i, f: (n_layer, i * n_f + f, 0)),
        ]
        args += [nw_in, nw_out]
        out_specs += [pl.BlockSpec((r_in, 2 * D_FF), slab), pl.BlockSpec((r_out, D_MODEL), slab)]
        out_shape += [jax.ShapeDtypeStruct((D_MODEL, 2 * D_FF), BF16),
                      jax.ShapeDtypeStruct((D_FF, D_MODEL), BF16)]
    body = functools.partial(_ffn_body, n_f=n_f, attn_in=attn_in is not None, ple=ple is not None,
                             final=final_g is not None, q_out=q_out is not None,
                             cast=cast_next is not None)
    return pl.pallas_call(
        body,
        grid=(n_t, n_f),
        in_specs=in_specs,
        out_specs=out_specs,
        out_shape=out_shape,
        scratch_shapes=[pltpu.VMEM((tm, D_MODEL), BF16)],
        compiler_params=_params(("parallel", "arbitrary")),
        name="ffn",
    )(*args)


def _conv_body(h_ref, g_ref, win_ref, bin_ref, wdw_ref, bdw_ref, lng_ref, lnb_ref,
               wout_ref, bout_ref, out_ref, ubuf_ref, cbuf_ref, sh_ref, *, ts):
    n_lb = D_MODEL // LANES
    halo = CONV_HALO
    first_tap_row = halo - (CONV_WIDTH - 1)

    @pl.when(pl.program_id(1) == 0)
    def _():
        ubuf_ref[:, 0:halo, :] = jnp.zeros((n_lb, halo, LANES), F32)

    x = h_ref[...]
    xn = _rms(x, g_ref[...]).astype(BF16)
    y = jnp.dot(xn, win_ref[...], preferred_element_type=F32) + bin_ref[...]
    u = y[:, :D_MODEL] * _sigmoid(y[:, D_MODEL:])
    for j in range(n_lb):
        ubuf_ref[j, halo:halo + ts, :] = u[:, j * LANES:(j + 1) * LANES]

    def lane_block(j, carry):
        for r in range(PACK):
            n = ts + (PACK if r < PACK - 1 else 0)
            sh_ref[r, 0:n, :] = ubuf_ref[j, pl.ds(first_tap_row + r, n), :].astype(BF16)
        for g in range(ts // PACK):
            acc = None
            for k in range(CONV_WIDTH):
                a, r = divmod(k, PACK)
                seg = sh_ref[r, (g + a) * PACK:(g + a + 1) * PACK, :]
                term = seg.astype(F32) * wdw_ref[j, k].astype(F32)
                acc = term if acc is None else acc + term
            cbuf_ref[j, g * PACK:(g + 1) * PACK, :] = acc
        return carry

    lax.fori_loop(0, n_lb, lane_block, 0)

    ubuf_ref[:, 0:halo, :] = ubuf_ref[:, ts:ts + halo, :]

    c = jnp.concatenate([cbuf_ref[j] for j in range(n_lb)], axis=1) + bdw_ref[...]
    mu = jnp.mean(c, axis=-1, keepdims=True)
    d = c - mu
    var = jnp.mean(d * d, axis=-1, keepdims=True)
    z = d * lax.rsqrt(var + LN_EPS) * lng_ref[...] + lnb_ref[...]
    z = (z * _sigmoid(z)).astype(BF16)
    out_ref[...] = x + jnp.dot(z, wout_ref[...], preferred_element_type=F32) + bout_ref[...]


def _conv_layer(h, batch, seq, layer, norm_g, w_in, b_in, w_dw, b_dw, ln_g, ln_b, w_out, b_out):
    ts = CONV_TS
    n_s = seq // ts
    n_lb = D_MODEL // LANES
    wdw = jnp.broadcast_to(
        w_dw.astype(BF16).reshape(CONV_WIDTH, n_lb, 1, LANES).transpose(1, 0, 2, 3),
        (n_lb, CONV_WIDTH, PACK, LANES))
    row = lambda b, s: (b * n_s + s, 0)
    fixed = lambda b, s: (0, 0)
    vec = pl.BlockSpec((1, D_MODEL), fixed)
    return pl.pallas_call(
        functools.partial(_conv_body, ts=ts),
        grid=(batch, n_s),
        in_specs=[
            pl.BlockSpec((ts, D_MODEL), row),
            vec,
            pl.BlockSpec((None, D_MODEL, 2 * D_MODEL), lambda b, s: (layer, 0, 0)),
            pl.BlockSpec((1, 2 * D_MODEL), fixed),
            pl.BlockSpec((n_lb, CONV_WIDTH, PACK, LANES), lambda b, s: (0, 0, 0, 0)),
            vec, vec, vec,
            pl.BlockSpec((None, D_MODEL, D_MODEL), lambda b, s: (layer, 0, 0)),
            vec,
        ],
        out_specs=pl.BlockSpec((ts, D_MODEL), row),
        out_shape=jax.ShapeDtypeStruct(h.shape, F32),
        scratch_shapes=[
            pltpu.VMEM((n_lb, CONV_HALO + ts, LANES), F32),
            pltpu.VMEM((n_lb, ts, LANES), F32),
            pltpu.VMEM((PACK, ts + PACK, LANES), BF16),
        ],
        compiler_params=_params(("parallel", "arbitrary")),
        name="conformer_conv",
    )(h, norm_g.reshape(1, -1), w_in, b_in.reshape(1, -1), wdw, b_dw.reshape(1, -1),
      ln_g.reshape(1, -1), ln_b.reshape(1, -1), w_out, b_out.reshape(1, -1))


def _bucket_tiles(t):
    assert t >= MAX_DISTANCE
    qp = np.arange(t)[:, None]
    kp = np.arange(t)[None, :]
    max_exact = NUM_BUCKETS // 2

    def bucket(n):
        nf = np.maximum(n, 1).astype(np.float32)
        large = max_exact + (np.log(nf / max_exact) / math.log(MAX_DISTANCE / max_exact)
                             * (NUM_BUCKETS - max_exact)).astype(np.int32)
        return np.where(n < max_exact, n, np.minimum(large, NUM_BUCKETS - 1))

    diag = np.where(kp <= qp, bucket(np.maximum(qp - kp, 0)), -1)
    left = bucket(qp + t - kp)
    return np.stack([diag, left]).astype(np.int32)


def _bias_body(rb_ref, bucket_ref, out_ref):
    h = pl.program_id(0)
    far = rb_ref[NUM_BUCKETS - 1, h]
    bk = bucket_ref[...]
    acc = jnp.full(bk.shape, NEG_INF, F32)
    for b in range(NUM_BUCKETS):
        acc = jnp.where(bk == b, rb_ref[b, h] - far, acc)
    out_ref[...] = acc


def _rel_bias_tiles(rel_bias, t):
    return pl.pallas_call(
        _bias_body,
        grid=(N_HEADS,),
        in_specs=[
            pl.BlockSpec(memory_space=pltpu.SMEM),
            pl.BlockSpec((2, t, t), lambda h: (0, 0, 0)),
        ],
        out_specs=pl.BlockSpec((None, 2, t, t), lambda h: (h, 0, 0, 0)),
        out_shape=jax.ShapeDtypeStruct((N_HEADS, 2, t, t), F32),
        compiler_params=_params(("parallel",)),
        name="rel_bias_tiles",
    )(rel_bias.astype(F32), jnp.asarray(_bucket_tiles(t)))


def _attn_body(q_ref, kt_ref, v_ref, bias_ref, lam_ref, sub_ref, out_ref, vaug_ref,
               *, t, n_q, lambda_init):
    seq = v_ref.shape[0]
    lq = lam_ref[...]
    lam = (jnp.exp(jnp.sum(lq[0:1] * lq[1:2], axis=-1, keepdims=True))
           - jnp.exp(jnp.sum(lq[2:3] * lq[3:4], axis=-1, keepdims=True)) + lambda_init)
    for hh in range(ATT_HEADS):
        vaug_ref[hh, :, :LANES] = v_ref[:, hh * LANES:(hh + 1) * LANES]
        vaug_ref[hh, :, LANES:] = jnp.ones((seq, LANES), BF16)

    for hh, i in [(hh, i) for hh in range(ATT_HEADS) for i in reversed(range(n_q))]:
        head = slice(hh * LANES, (hh + 1) * LANES)
        q = q_ref[i * t:(i + 1) * t, head]
        lane = lax.broadcasted_iota(jnp.int32, q.shape, 1)
        zero = jnp.zeros_like(q)
        q12 = jnp.concatenate([jnp.where(lane < HEAD_DIM, q, zero),
                               jnp.where(lane >= HEAD_DIM, q, zero)], axis=0)
        pieces = []
        if i >= 2:
            pieces.append((0, (i - 1) * t, None))
        if i >= 1:
            pieces.append(((i - 1) * t, i * t, 1))
        pieces.append((i * t, (i + 1) * t, 0))

        scores = []
        colmax = None
        for c0, c1, bias_idx in pieces:
            s = jnp.dot(q12, kt_ref[head, c0:c1], preferred_element_type=F32)
            if bias_idx is not None:
                s = (s.reshape(2, t, t) + bias_ref[hh, bias_idx][None]).reshape(2 * t, t)
            for c in range((c1 - c0) // LANES):
                blk = s[:, c * LANES:(c + 1) * LANES]
                colmax = blk if colmax is None else jnp.maximum(colmax, blk)
            scores.append(s)
        m = jnp.max(colmax, axis=1, keepdims=True)

        acc = None
        for (c0, c1, _), s in zip(pieces, scores):
            p = jnp.exp(s - m).astype(BF16)
            d = jnp.dot(p, vaug_ref[hh, c0:c1, :], preferred_element_type=F32)
            acc = d if acc is None else acc + d
        o12 = acc[:, :LANES] / acc[:, LANES:]
        o = o12[:t] - lam * o12[t:]
        out_ref[i * t:(i + 1) * t, head] = (
            _rms(o, sub_ref[...]) * (1.0 - lambda_init)).astype(out_ref.dtype)


def _attention(q, kt, v, bias_tiles, lam_rows, subln, batch, seq, lambda_init):
    t = ATT_T
    n_q = seq // t
    q3 = q.reshape(batch, seq, QK_DIM)
    width = ATT_HEADS * LANES
    head_cols = pl.BlockSpec((None, seq, width), lambda b, h: (b, 0, h))
    out = pl.pallas_call(
        functools.partial(_attn_body, t=t, n_q=n_q, lambda_init=lambda_init),
        grid=(batch, N_HEADS // ATT_HEADS),
        in_specs=[
            head_cols,
            pl.BlockSpec((None, width, seq), lambda b, h: (b, h, 0)),
            head_cols,
            pl.BlockSpec((ATT_HEADS, 2, t, t), lambda b, h: (h, 0, 0, 0)),
            pl.BlockSpec((4, HEAD_DIM), lambda b, h: (0, 0)),
            pl.BlockSpec((1, LANES), lambda b, h: (0, 0)),
        ],
        out_specs=head_cols,
        out_shape=jax.ShapeDtypeStruct((batch, seq, V_DIM), BF16),
        scratch_shapes=[pltpu.VMEM((ATT_HEADS, seq, 2 * LANES), BF16)],
        compiler_params=_params(("parallel", "parallel")),
        name="diff_attention",
    )(q3, kt, v, bias_tiles, lam_rows, subln.reshape(1, LANES))
    return out.reshape(batch * seq, V_DIM)


def _kv_body(h_ref, g_ref, wkt_ref, wv_ref, kt_ref, v_ref):
    xn = _rms(h_ref[...], g_ref[...]).astype(BF16)
    kt = lax.dot_general(wkt_ref[...], xn, (((1,), (1,)), ((), ())),
                         preferred_element_type=F32)
    kt_ref[...] = kt.astype(BF16)
    v_ref[...] = jnp.dot(xn, wv_ref[...], preferred_element_type=F32).astype(BF16)


def _shared_kv(h, batch, seq, norm_g, w_kt, w_v):
    tm = PROJ_TM
    n_s = seq // tm
    fixed = lambda b, s: (0, 0)
    return pl.pallas_call(
        _kv_body,
        grid=(batch, n_s),
        in_specs=[
            pl.BlockSpec((tm, D_MODEL), lambda b, s: (b * n_s + s, 0)),
            pl.BlockSpec((1, D_MODEL), fixed),
            pl.BlockSpec((QK_DIM, D_MODEL), fixed),
            pl.BlockSpec((D_MODEL, V_DIM), fixed),
        ],
        out_specs=[
            pl.BlockSpec((None, QK_DIM, tm), lambda b, s: (b, 0, s)),
            pl.BlockSpec((None, tm, V_DIM), lambda b, s: (b, s, 0)),
        ],
        out_shape=[
            jax.ShapeDtypeStruct((batch, QK_DIM, seq), BF16),
            jax.ShapeDtypeStruct((batch, seq, V_DIM), BF16),
        ],
        compiler_params=_params(("parallel", "parallel")),
        name="shared_kv",
    )(h, norm_g.reshape(1, D_MODEL), w_kt, w_v)


def _head_major(w):
    lead = w.shape[:-1]
    w = w.reshape(*lead, 2, N_HEADS, HEAD_DIM)
    return jnp.swapaxes(w, -3, -2).reshape(*lead, QK_DIM)


def kernel(x, p, ffn1_norm, ffn1_w_in, ffn1_w_out, mix_norm, ffn2_norm, ffn2_w_in, ffn2_w_out,
           ple_norm, ple_w_gate, ple_w_proj, conv_w_in, conv_b_in, conv_w_dw, conv_b_dw,
           conv_ln_g, conv_ln_b, conv_w_out, conv_b_out, kv_norm, w_kv, attn_w_q,
           attn_lq1, attn_lk1, attn_lq2, attn_lk2, attn_subln, attn_w_o, rel_bias, final_norm):
    batch, seq, _ = x.shape
    tokens = batch * seq
    h = x.reshape(tokens, D_MODEL)
    p2 = p.reshape(DEPTH, tokens, PLE_DIM)
    bias_tiles = _rel_bias_tiles(rel_bias, ATT_T)
    w_kt = _head_major(w_kv[:, :QK_DIM]).T.astype(BF16)
    w_v = w_kv[:, QK_DIM:].astype(BF16)
    ple_w_gate, ple_w_proj = ple_w_gate.astype(BF16), ple_w_proj.astype(BF16)
    conv_w_in, conv_w_out = conv_w_in.astype(BF16), conv_w_out.astype(BF16)
    attn_w_o = attn_w_o.astype(BF16)
    w_q = (_head_major(attn_w_q) * (HEAD_DIM ** -0.5)).astype(BF16)
    w_in, w_out = ffn1_w_in[0:1].astype(BF16), ffn1_w_out[0:1].astype(BF16)
    kt = v = o = None
    for i in range(DEPTH):
        j = i - N_A
        if i == N_A:
            kt, v = _shared_kv(h, batch, seq, kv_norm, w_kt, w_v)
        res = _ffn(h, 0, ffn1_norm[i], w_in, w_out,
                   q_out=(mix_norm[i], j, w_q) if i >= N_A else None,
                   cast_next=(ffn2_w_in, ffn2_w_out, i))
        h, w_in, w_out = res[0], res[-2][None], res[-1][None]
        if i < N_A:
            h = _conv_layer(h, batch, seq, i, mix_norm[i], conv_w_in, conv_b_in[i],
                            conv_w_dw[i], conv_b_dw[i], conv_ln_g[i], conv_ln_b[i],
                            conv_w_out, conv_b_out[i])
        else:
            lambda_init = 0.8 - 0.6 * math.exp(-0.3 * i)
            lam_rows = jnp.stack([attn_lq1[j], attn_lk1[j], attn_lq2[j], attn_lk2[j]]).astype(F32)
            o = _attention(res[1], kt, v, bias_tiles, lam_rows, attn_subln[j], batch, seq,
                           lambda_init)
        last = i == DEPTH - 1
        res = _ffn(h, 0, ffn2_norm[i], w_in, w_out,
                   attn_in=(o, j, attn_w_o) if i >= N_A else None,
                   ple=(ple_norm[i], i, ple_w_gate, p2, ple_w_proj),
                   final_g=final_norm if last else None,
                   cast_next=None if last else (ffn1_w_in, ffn1_w_out, i + 1))
        h = res[0]
        if not last:
            w_in, w_out = res[-2][None], res[-1][None]
    return h.reshape(batch, seq, D_MODEL)
```

```python
import functools
import math

import jax
import jax.numpy as jnp
import numpy as np
from jax import lax
from jax.experimental import pallas as pl
from jax.experimental.pallas import tpu as pltpu

D_MODEL = 1024
DEPTH = 4
N_A = DEPTH // 2
D_FF = 4 * D_MODEL
CONV_WIDTH = 31
HEAD_DIM = 64
N_HEADS = D_MODEL // (2 * HEAD_DIM)
QK_DIM = 2 * N_HEADS * HEAD_DIM
V_DIM = N_HEADS * 2 * HEAD_DIM
NUM_BUCKETS = 32
MAX_DISTANCE = 128
PLE_DIM = 256
RMS_EPS = 1e-6
LN_EPS = 1e-5
NEG_INF = -1e30

LANES = 128
SUBLANES = 8
PACK = 16
VMEM_LIMIT_BYTES = 56 * 1024 * 1024

FFN_TM = 1024
FFN_TF = 1024
FFN_SUB = 512
FFN_EDGE_PARTS = 4
PROJ_TM = 1024
CONV_TS = 512
CONV_HALO = 32
CONV_PARTS = 2
CAST_STEPS = 8
ATT_T = 256
ATT_HEADS = 2

F32 = jnp.float32
BF16 = jnp.bfloat16


def _rms(x, g):
    ms = jnp.mean(x * x, axis=-1, keepdims=True)
    return x * lax.rsqrt(ms + RMS_EPS) * g


def _sigmoid(x):
    return 1.0 / (1.0 + jnp.exp(-x))


def _params(sem):
    return pltpu.CompilerParams(dimension_semantics=sem, vmem_limit_bytes=VMEM_LIMIT_BYTES)


def _ffn_body(*refs, n_f, attn_in, ple, final, q_out, cast):
    it = iter(refs)
    h_ref, g_ref, wg_ref, wu_ref, wo_ref = (next(it) for _ in range(5))
    if attn_in:
        o_ref, wproj_ref = next(it), next(it)
    if ple:
        pn_ref, pwg_ref, p_ref, pwp_ref = (next(it) for _ in range(4))
    if final:
        fn_ref = next(it)
    if q_out:
        qg_ref, wq_ref = next(it), next(it)
    if cast:
        nwin_ref, nwout_ref = next(it), next(it)
    out_ref = next(it)
    if q_out:
        q_ref = next(it)
    if cast:
        cwin_ref, cwout_ref = next(it), next(it)
    xn_ref = next(it)

    f = pl.program_id(1)

    rp = h_ref.shape[0] // FFN_EDGE_PARTS
    parts = [slice(k * rp, (k + 1) * rp) for k in range(FFN_EDGE_PARTS)]

    @pl.when(f == 0)
    def _():
        for rows in parts:
            x = h_ref[rows, :]
            if attn_in:
                x = x + jnp.dot(o_ref[rows, :], wproj_ref[...], preferred_element_type=F32)
            xn_ref[rows, :] = _rms(x, g_ref[...]).astype(BF16)
            out_ref[rows, :] = x

    if cast:
        cwin_ref[...] = nwin_ref[...].astype(BF16)
        cwout_ref[...] = nwout_ref[...].astype(BF16)

    xn = xn_ref[...]
    part = None
    for c in range(FFN_TF // FFN_SUB):
        cols = slice(c * FFN_SUB, (c + 1) * FFN_SUB)
        gate = jnp.dot(xn, wg_ref[:, cols], preferred_element_type=F32)
        up = jnp.dot(xn, wu_ref[:, cols], preferred_element_type=F32)
        act = (gate * _sigmoid(gate) * up).astype(BF16)
        d = jnp.dot(act, wo_ref[cols, :], preferred_element_type=F32)
        part = d if part is None else part + d
    out_ref[...] += 0.5 * part

    if ple or final or q_out:
        @pl.when(f == n_f - 1)
        def _():
            for rows in parts:
                hn = out_ref[rows, :]
                if ple:
                    xg = _rms(hn, pn_ref[...]).astype(BF16)
                    gt = _sigmoid(jnp.dot(xg, pwg_ref[...], preferred_element_type=F32))
                    pr = jnp.dot(p_ref[rows, :].astype(BF16), pwp_ref[...],
                                 preferred_element_type=F32)
                    hn = hn + gt * pr
                if final:
                    hn = _rms(hn, fn_ref[...])
                if ple or final:
                    out_ref[rows, :] = hn
                if q_out:
                    xq = _rms(hn, qg_ref[...]).astype(BF16)
                    q_ref[rows, :] = jnp.dot(
                        xq, wq_ref[...], preferred_element_type=F32).astype(BF16)


def _ffn(h, layer, norm_g, w_in, w_out, attn_in=None, ple=None, final_g=None, q_out=None,
         cast_next=None):
    t = h.shape[0]
    tm, tf = FFN_TM, FFN_TF
    n_t, n_f = t // tm, D_FF // tf
    row = lambda i, f: (i, 0)
    fixed = lambda i, f: (0, 0)
    once = pl.Buffered(1)
    vec = pl.BlockSpec((1, D_MODEL), fixed, pipeline_mode=once)

    def square(lyr):
        return pl.BlockSpec((None, D_MODEL, D_MODEL), lambda i, f: (lyr, 0, 0), pipeline_mode=once)

    in_specs = [
        pl.BlockSpec((tm, D_MODEL), row),
        vec,
        pl.BlockSpec((None, D_MODEL, tf), lambda i, f: (layer, 0, f)),
        pl.BlockSpec((None, D_MODEL, tf), lambda i, f: (layer, 0, f + n_f)),
        pl.BlockSpec((None, tf, D_MODEL), lambda i, f: (layer, f, 0)),
    ]
    args = [h, norm_g.reshape(1, D_MODEL), w_in, w_in, w_out]
    if attn_in is not None:
        o, proj_layer, w_proj = attn_in
        in_specs += [pl.BlockSpec((tm, V_DIM), row), square(proj_layer)]
        args += [o, w_proj]
    if ple is not None:
        pn, ple_layer, pwg, p, pwp = ple
        in_specs += [
            vec,
            square(ple_layer),
            pl.BlockSpec((None, tm, PLE_DIM), lambda i, f: (ple_layer, i, 0)),
            pl.BlockSpec((None, PLE_DIM, D_MODEL), lambda i, f: (ple_layer, 0, 0),
                         pipeline_mode=once),
        ]
        args += [pn.reshape(1, D_MODEL), pwg, p, pwp]
    if final_g is not None:
        in_specs.append(vec)
        args.append(final_g.reshape(1, D_MODEL))
    if q_out is not None:
        g_q, q_layer, w_q = q_out
        in_specs += [vec, square(q_layer)]
        args += [g_q.reshape(1, D_MODEL), w_q]
    out_specs = [pl.BlockSpec((tm, D_MODEL), row)]
    out_shape = [jax.ShapeDtypeStruct((t, D_MODEL), F32)]
    if q_out is not None:
        out_specs.append(pl.BlockSpec((tm, QK_DIM), row))
        out_shape.append(jax.ShapeDtypeStruct((t, QK_DIM), BF16))
    if cast_next is not None:
        nw_in, nw_out, n_layer = cast_next
        steps = n_t * n_f
        r_in, r_out = D_MODEL // steps, D_FF // steps
        slab = lambda i, f: (i * n_f + f, 0)
        in_specs += [
            pl.BlockSpec((None, r_in, 2 * D_FF), lambda i, f: (n_layer, i * n_f + f, 0)),
            pl.BlockSpec((None, r_out, D_MODEL), lambda i, f: (n_layer, i * n_f + f, 0)),
        ]
        args += [nw_in, nw_out]
        out_specs += [pl.BlockSpec((r_in, 2 * D_FF), slab), pl.BlockSpec((r_out, D_MODEL), slab)]
        out_shape += [jax.ShapeDtypeStruct((D_MODEL, 2 * D_FF), BF16),
                      jax.ShapeDtypeStruct((D_FF, D_MODEL), BF16)]
    body = functools.partial(_ffn_body, n_f=n_f, attn_in=attn_in is not None, ple=ple is not None,
                             final=final_g is not None, q_out=q_out is not None,
                             cast=cast_next is not None)
    return pl.pallas_call(
        body,
        grid=(n_t, n_f),
        in_specs=in_specs,
        out_specs=out_specs,
        out_shape=out_shape,
        scratch_shapes=[pltpu.VMEM((tm, D_MODEL), BF16)],
        compiler_params=_params(("parallel", "arbitrary")),
        name="ffn",
    )(*args)


def _conv_body(h_ref, g_ref, win_ref, bin_ref, wdw_ref, bdw_ref, lng_ref, lnb_ref,
               wout_ref, bout_ref, out_ref, ubuf_ref, cbuf_ref, sh_ref, *, ts):
    n_lb = D_MODEL // LANES
    halo = CONV_HALO
    first_tap_row = halo - (CONV_WIDTH - 1)

    @pl.when(pl.program_id(1) == 0)
    def _():
        ubuf_ref[:, 0:halo, :] = jnp.zeros((n_lb, halo, LANES), F32)

    rp = ts // CONV_PARTS
    for part in range(CONV_PARTS):
        r0 = part * rp
        xn = _rms(h_ref[r0:r0 + rp, :], g_ref[...]).astype(BF16)
        y = jnp.dot(xn, win_ref[...], preferred_element_type=F32) + bin_ref[...]
        u = y[:, :D_MODEL] * _sigmoid(y[:, D_MODEL:])
        for j in range(n_lb):
            ubuf_ref[j, halo + r0:halo + r0 + rp, :] = u[:, j * LANES:(j + 1) * LANES]

    def lane_block(j, carry):
        for r in range(PACK):
            n = ts + (PACK if r < PACK - 1 else 0)
            sh_ref[r, 0:n, :] = ubuf_ref[j, pl.ds(first_tap_row + r, n), :].astype(BF16)
        for g in range(ts // PACK):
            acc = None
            for k in range(CONV_WIDTH):
                a, r = divmod(k, PACK)
                seg = sh_ref[r, (g + a) * PACK:(g + a + 1) * PACK, :]
                term = seg.astype(F32) * wdw_ref[j, k].astype(F32)
                acc = term if acc is None else acc + term
            cbuf_ref[j, g * PACK:(g + 1) * PACK, :] = acc
        return carry

    lax.fori_loop(0, n_lb, lane_block, 0)

    ubuf_ref[:, 0:halo, :] = ubuf_ref[:, ts:ts + halo, :]

    for part in range(CONV_PARTS):
        r0 = part * rp
        c = jnp.concatenate([cbuf_ref[j, r0:r0 + rp, :] for j in range(n_lb)], axis=1)
        c = c + bdw_ref[...]
        mu = jnp.mean(c, axis=-1, keepdims=True)
        d = c - mu
        var = jnp.mean(d * d, axis=-1, keepdims=True)
        z = d * lax.rsqrt(var + LN_EPS) * lng_ref[...] + lnb_ref[...]
        z = (z * _sigmoid(z)).astype(BF16)
        out_ref[r0:r0 + rp, :] = (h_ref[r0:r0 + rp, :] + bout_ref[...]
                                  + jnp.dot(z, wout_ref[...], preferred_element_type=F32))


def _conv_layer(h, batch, seq, layer, norm_g, w_in, b_in, w_dw, b_dw, ln_g, ln_b, w_out, b_out):
    ts = CONV_TS
    n_s = seq // ts
    n_lb = D_MODEL // LANES
    wdw = jnp.broadcast_to(
        w_dw.astype(BF16).reshape(CONV_WIDTH, n_lb, 1, LANES).transpose(1, 0, 2, 3),
        (n_lb, CONV_WIDTH, PACK, LANES))
    row = lambda b, s: (b * n_s + s, 0)
    fixed = lambda b, s: (0, 0)
    vec = pl.BlockSpec((1, D_MODEL), fixed)
    return pl.pallas_call(
        functools.partial(_conv_body, ts=ts),
        grid=(batch, n_s),
        in_specs=[
            pl.BlockSpec((ts, D_MODEL), row),
            vec,
            pl.BlockSpec((None, D_MODEL, 2 * D_MODEL), lambda b, s: (layer, 0, 0)),
            pl.BlockSpec((1, 2 * D_MODEL), fixed),
            pl.BlockSpec((n_lb, CONV_WIDTH, PACK, LANES), lambda b, s: (0, 0, 0, 0)),
            vec, vec, vec,
            pl.BlockSpec((None, D_MODEL, D_MODEL), lambda b, s: (layer, 0, 0)),
            vec,
        ],
        out_specs=pl.BlockSpec((ts, D_MODEL), row),
        out_shape=jax.ShapeDtypeStruct(h.shape, F32),
        scratch_shapes=[
            pltpu.VMEM((n_lb, CONV_HALO + ts, LANES), F32),
            pltpu.VMEM((n_lb, ts, LANES), F32),
            pltpu.VMEM((PACK, ts + PACK, LANES), BF16),
        ],
        compiler_params=_params(("parallel", "arbitrary")),
        name="conformer_conv",
    )(h, norm_g.reshape(1, -1), w_in, b_in.reshape(1, -1), wdw, b_dw.reshape(1, -1),
      ln_g.reshape(1, -1), ln_b.reshape(1, -1), w_out, b_out.reshape(1, -1))


def _bucket_tiles(t):
    assert t >= MAX_DISTANCE
    qp = np.arange(t)[:, None]
    kp = np.arange(t)[None, :]
    max_exact = NUM_BUCKETS // 2

    def bucket(n):
        nf = np.maximum(n, 1).astype(np.float32)
        large = max_exact + (np.log(nf / max_exact) / math.log(MAX_DISTANCE / max_exact)
                             * (NUM_BUCKETS - max_exact)).astype(np.int32)
        return np.where(n < max_exact, n, np.minimum(large, NUM_BUCKETS - 1))

    diag = np.where(kp <= qp, bucket(np.maximum(qp - kp, 0)), -1)
    left = bucket(qp + t - kp)
    return np.stack([diag, left]).astype(np.int32)


def _bias_body(rb_ref, bucket_ref, out_ref):
    h = pl.program_id(0)
    far = rb_ref[NUM_BUCKETS - 1, h]
    bk = bucket_ref[...]
    acc = jnp.full(bk.shape, NEG_INF, F32)
    for b in range(NUM_BUCKETS):
        acc = jnp.where(bk == b, rb_ref[b, h] - far, acc)
    out_ref[...] = acc


def _rel_bias_tiles(rel_bias, t):
    return pl.pallas_call(
        _bias_body,
        grid=(N_HEADS,),
        in_specs=[
            pl.BlockSpec(memory_space=pltpu.SMEM),
            pl.BlockSpec((2, t, t), lambda h: (0, 0, 0)),
        ],
        out_specs=pl.BlockSpec((None, 2, t, t), lambda h: (h, 0, 0, 0)),
        out_shape=jax.ShapeDtypeStruct((N_HEADS, 2, t, t), F32),
        compiler_params=_params(("parallel",)),
        name="rel_bias_tiles",
    )(rel_bias.astype(F32), jnp.asarray(_bucket_tiles(t)))


def _attn_body(q_ref, kt_ref, v_ref, bias_ref, lam_ref, sub_ref, out_ref, vaug_ref,
               *, t, n_q, lambda_init):
    seq = v_ref.shape[0]
    lq = lam_ref[...]
    lam = (jnp.exp(jnp.sum(lq[0:1] * lq[1:2], axis=-1, keepdims=True))
           - jnp.exp(jnp.sum(lq[2:3] * lq[3:4], axis=-1, keepdims=True)) + lambda_init)
    for hh in range(ATT_HEADS):
        vaug_ref[hh, :, :LANES] = v_ref[:, hh * LANES:(hh + 1) * LANES]
        vaug_ref[hh, :, LANES:] = jnp.ones((seq, LANES), BF16)

    for hh, i in [(hh, i) for hh in range(ATT_HEADS) for i in reversed(range(n_q))]:
        head = slice(hh * LANES, (hh + 1) * LANES)
        q = q_ref[i * t:(i + 1) * t, head]
        lane = lax.broadcasted_iota(jnp.int32, q.shape, 1)
        zero = jnp.zeros_like(q)
        q12 = jnp.concatenate([jnp.where(lane < HEAD_DIM, q, zero),
                               jnp.where(lane >= HEAD_DIM, q, zero)], axis=0)
        pieces = []
        if i >= 2:
            pieces.append((0, (i - 1) * t, None))
        if i >= 1:
            pieces.append(((i - 1) * t, i * t, 1))
        pieces.append((i * t, (i + 1) * t, 0))

        scores = []
        colmax = None
        for c0, c1, bias_idx in pieces:
            s = jnp.dot(q12, kt_ref[head, c0:c1], preferred_element_type=F32)
            if bias_idx is not None:
                s = (s.reshape(2, t, t) + bias_ref[hh, bias_idx][None]).reshape(2 * t, t)
            for c in range((c1 - c0) // LANES):
                blk = s[:, c * LANES:(c + 1) * LANES]
                colmax = blk if colmax is None else jnp.maximum(colmax, blk)
            scores.append(s)
        m = jnp.max(colmax, axis=1, keepdims=True)

        acc = None
        for (c0, c1, _), s in zip(pieces, scores):
            p = jnp.exp(s - m).astype(BF16)
            d = jnp.dot(p, vaug_ref[hh, c0:c1, :], preferred_element_type=F32)
            acc = d if acc is None else acc + d
        o12 = acc[:, :LANES] / acc[:, LANES:]
        o = o12[:t] - lam * o12[t:]
        out_ref[i * t:(i + 1) * t, head] = (
            _rms(o, sub_ref[...]) * (1.0 - lambda_init)).astype(out_ref.dtype)


def _attention(q, kt, v, bias_tiles, lam_rows, subln, batch, seq, lambda_init):
    t = ATT_T
    n_q = seq // t
    q3 = q.reshape(batch, seq, QK_DIM)
    width = ATT_HEADS * LANES
    head_cols = pl.BlockSpec((None, seq, width), lambda b, h: (b, 0, h))
    out = pl.pallas_call(
        functools.partial(_attn_body, t=t, n_q=n_q, lambda_init=lambda_init),
        grid=(batch, N_HEADS // ATT_HEADS),
        in_specs=[
            head_cols,
            pl.BlockSpec((None, width, seq), lambda b, h: (b, h, 0)),
            head_cols,
            pl.BlockSpec((ATT_HEADS, 2, t, t), lambda b, h: (h, 0, 0, 0)),
            pl.BlockSpec((4, HEAD_DIM), lambda b, h: (0, 0)),
            pl.BlockSpec((1, LANES), lambda b, h: (0, 0)),
        ],
        out_specs=head_cols,
        out_shape=jax.ShapeDtypeStruct((batch, seq, V_DIM), BF16),
        scratch_shapes=[pltpu.VMEM((ATT_HEADS, seq, 2 * LANES), BF16)],
        compiler_params=_params(("parallel", "parallel")),
        name="diff_attention",
    )(q3, kt, v, bias_tiles, lam_rows, subln.reshape(1, LANES))
    return out.reshape(batch * seq, V_DIM)


def _kv_body(h_ref, g_ref, wkt_ref, wv_ref, kt_ref, v_ref):
    xn = _rms(h_ref[...], g_ref[...]).astype(BF16)
    kt = lax.dot_general(wkt_ref[...], xn, (((1,), (1,)), ((), ())),
                         preferred_element_type=F32)
    kt_ref[...] = kt.astype(BF16)
    v_ref[...] = jnp.dot(xn, wv_ref[...], preferred_element_type=F32).astype(BF16)


def _shared_kv(h, batch, seq, norm_g, w_kt, w_v):
    tm = PROJ_TM
    n_s = seq // tm
    fixed = lambda b, s: (0, 0)
    return pl.pallas_call(
        _kv_body,
        grid=(batch, n_s),
        in_specs=[
            pl.BlockSpec((tm, D_MODEL), lambda b, s: (b * n_s + s, 0)),
            pl.BlockSpec((1, D_MODEL), fixed),
            pl.BlockSpec((QK_DIM, D_MODEL), fixed),
            pl.BlockSpec((D_MODEL, V_DIM), fixed),
        ],
        out_specs=[
            pl.BlockSpec((None, QK_DIM, tm), lambda b, s: (b, 0, s)),
            pl.BlockSpec((None, tm, V_DIM), lambda b, s: (b, s, 0)),
        ],
        out_shape=[
            jax.ShapeDtypeStruct((batch, QK_DIM, seq), BF16),
            jax.ShapeDtypeStruct((batch, seq, V_DIM), BF16),
        ],
        compiler_params=_params(("parallel", "parallel")),
        name="shared_kv",
    )(h, norm_g.reshape(1, D_MODEL), w_kt, w_v)


def _cast_body(win_ref, wout_ref, cwin_ref, cwout_ref):
    cwin_ref[...] = win_ref[...].astype(BF16)
    cwout_ref[...] = wout_ref[...].astype(BF16)


def _cast_ffn_weights(w_in, w_out, layer):
    n = CAST_STEPS
    r_in, r_out = D_MODEL // n, D_FF // n
    slab = lambda s: (0, s, 0)
    return pl.pallas_call(
        _cast_body,
        grid=(n,),
        in_specs=[
            pl.BlockSpec((1, r_in, 2 * D_FF), lambda s: (layer, s, 0)),
            pl.BlockSpec((1, r_out, D_MODEL), lambda s: (layer, s, 0)),
        ],
        out_specs=[pl.BlockSpec((1, r_in, 2 * D_FF), slab), pl.BlockSpec((1, r_out, D_MODEL), slab)],
        out_shape=[jax.ShapeDtypeStruct((1, D_MODEL, 2 * D_FF), BF16),
                   jax.ShapeDtypeStruct((1, D_FF, D_MODEL), BF16)],
        compiler_params=_params(("parallel",)),
        name="cast_ffn_weights",
    )(w_in, w_out)


def _head_major(w):
    lead = w.shape[:-1]
    w = w.reshape(*lead, 2, N_HEADS, HEAD_DIM)
    return jnp.swapaxes(w, -3, -2).reshape(*lead, QK_DIM)


def kernel(x, p, ffn1_norm, ffn1_w_in, ffn1_w_out, mix_norm, ffn2_norm, ffn2_w_in, ffn2_w_out,
           ple_norm, ple_w_gate, ple_w_proj, conv_w_in, conv_b_in, conv_w_dw, conv_b_dw,
           conv_ln_g, conv_ln_b, conv_w_out, conv_b_out, kv_norm, w_kv, attn_w_q,
           attn_lq1, attn_lk1, attn_lq2, attn_lk2, attn_subln, attn_w_o, rel_bias, final_norm):
    batch, seq, _ = x.shape
    tokens = batch * seq
    h = x.reshape(tokens, D_MODEL)
    p2 = p.reshape(DEPTH, tokens, PLE_DIM)
    bias_tiles = _rel_bias_tiles(rel_bias, ATT_T)
    w_kt = _head_major(w_kv[:, :QK_DIM]).T.astype(BF16)
    w_v = w_kv[:, QK_DIM:].astype(BF16)
    ple_w_gate, ple_w_proj = ple_w_gate.astype(BF16), ple_w_proj.astype(BF16)
    conv_w_in, conv_w_out = conv_w_in.astype(BF16), conv_w_out.astype(BF16)
    attn_w_o = attn_w_o.astype(BF16)
    w_q = (_head_major(attn_w_q) * (HEAD_DIM ** -0.5)).astype(BF16)
    w_in, w_out = _cast_ffn_weights(ffn1_w_in, ffn1_w_out, 0)
    kt = v = o = None
    for i in range(DEPTH):
        j = i - N_A
        if i == N_A:
            kt, v = _shared_kv(h, batch, seq, kv_norm, w_kt, w_v)
        res = _ffn(h, 0, ffn1_norm[i], w_in, w_out,
                   q_out=(mix_norm[i], j, w_q) if i >= N_A else None,
                   cast_next=(ffn2_w_in, ffn2_w_out, i))
        h, w_in, w_out = res[0], res[-2][None], res[-1][None]
        if i < N_A:
            h = _conv_layer(h, batch, seq, i, mix_norm[i], conv_w_in, conv_b_in[i],
                            conv_w_dw[i], conv_b_dw[i], conv_ln_g[i], conv_ln_b[i],
                            conv_w_out, conv_b_out[i])
        else:
            lambda_init = 0.8 - 0.6 * math.exp(-0.3 * i)
            lam_rows = jnp.stack([attn_lq1[j], attn_lk1[j], attn_lq2[j], attn_lk2[j]]).astype(F32)
            o = _attention(res[1], kt, v, bias_tiles, lam_rows, attn_subln[j], batch, seq,
                           lambda_init)
        last = i == DEPTH - 1
        res = _ffn(h, 0, ffn2_norm[i], w_in, w_out,
                   attn_in=(o, j, attn_w_o) if i >= N_A else None,
                   ple=(ple_norm[i], i, ple_w_gate, p2, ple_w_proj),
                   final_g=final_norm if last else None,
                   cast_next=None if last else (ffn1_w_in, ffn1_w_out, i + 1))
        h = res[0]
        if not last:
            w_in, w_out = res[-2][None], res[-1][None]
    return h.reshape(batch, seq, D_MODEL)
```

```python
import functools
import math

import jax
import jax.numpy as jnp
import numpy as np
from jax import lax
from jax.experimental import pallas as pl
from jax.experimental.pallas import tpu as pltpu

D_MODEL = 1024
DEPTH = 4
N_A = DEPTH // 2
D_FF = 4 * D_MODEL
CONV_WIDTH = 31
HEAD_DIM = 64
N_HEADS = D_MODEL // (2 * HEAD_DIM)
QK_DIM = 2 * N_HEADS * HEAD_DIM
V_DIM = N_HEADS * 2 * HEAD_DIM
NUM_BUCKETS = 32
MAX_DISTANCE = 128
PLE_DIM = 256
RMS_EPS = 1e-6
LN_EPS = 1e-5
NEG_INF = -1e30

LANES = 128
SUBLANES = 8
PACK = 16
VMEM_LIMIT_BYTES = 56 * 1024 * 1024

FFN_TM = 1024
FFN_TF = 1024
FFN_SUB = 512
PROJ_TM = 1024
CONV_TS = 512
CONV_HALO = 32
ATT_T = 256
ATT_HEADS = 2

F32 = jnp.float32
BF16 = jnp.bfloat16


def _rms(x, g):
    ms = jnp.mean(x * x, axis=-1, keepdims=True)
    return x * lax.rsqrt(ms + RMS_EPS) * g


def _sigmoid(x):
    return 1.0 / (1.0 + jnp.exp(-x))


def _params(sem):
    return pltpu.CompilerParams(dimension_semantics=sem, vmem_limit_bytes=VMEM_LIMIT_BYTES)


def _ffn_body(*refs, n_f, attn_in, ple, final, q_out, cast):
    it = iter(refs)
    h_ref, g_ref, wg_ref, wu_ref, wo_ref = (next(it) for _ in range(5))
    if attn_in:
        o_ref, wproj_ref = next(it), next(it)
    if ple:
        pn_ref, pwg_ref, p_ref, pwp_ref = (next(it) for _ in range(4))
    if final:
        fn_ref = next(it)
    if q_out:
        qg_ref, wq_ref = next(it), next(it)
    if cast:
        nwin_ref, nwout_ref = next(it), next(it)
    out_ref = next(it)
    if q_out:
        q_ref = next(it)
    if cast:
        cwin_ref, cwout_ref = next(it), next(it)
    xn_ref = next(it)

    f = pl.program_id(1)

    if cast:
        cwin_ref[...] = nwin_ref[...].astype(BF16)
        cwout_ref[...] = nwout_ref[...].astype(BF16)

    def step(first):
        if first:
            x = h_ref[...]
            if attn_in:
                x = x + jnp.dot(o_ref[...], wproj_ref[...], preferred_element_type=F32)
                out_ref[...] = x
            xn = _rms(x, g_ref[...]).astype(BF16)
            xn_ref[...] = xn
        else:
            xn = xn_ref[...]
        part = None
        for c in range(FFN_TF // FFN_SUB):
            cols = slice(c * FFN_SUB, (c + 1) * FFN_SUB)
            gate = jnp.dot(xn, wg_ref[:, cols], preferred_element_type=F32)
            up = jnp.dot(xn, wu_ref[:, cols], preferred_element_type=F32)
            act = (gate * _sigmoid(gate) * up).astype(BF16)
            d = jnp.dot(act, wo_ref[cols, :], preferred_element_type=F32)
            part = d if part is None else part + d
        if first and not attn_in:
            out_ref[...] = h_ref[...] + 0.5 * part
        else:
            out_ref[...] += 0.5 * part

    pl.when(f == 0)(functools.partial(step, True))
    pl.when(f > 0)(functools.partial(step, False))

    @pl.when(f == n_f - 1)
    def _():
        hn = out_ref[...]
        if ple:
            xg = _rms(hn, pn_ref[...]).astype(BF16)
            gt = _sigmoid(jnp.dot(xg, pwg_ref[...], preferred_element_type=F32))
            pr = jnp.dot(p_ref[...].astype(BF16), pwp_ref[...], preferred_element_type=F32)
            hn = hn + gt * pr
        if final:
            hn = _rms(hn, fn_ref[...])
        out_ref[...] = hn
        if q_out:
            xq = _rms(hn, qg_ref[...]).astype(BF16)
            q_ref[...] = jnp.dot(xq, wq_ref[...], preferred_element_type=F32).astype(BF16)


def _ffn(h, layer, norm_g, w_in, w_out, attn_in=None, ple=None, final_g=None, q_out=None,
         cast_next=None):
    t = h.shape[0]
    tm, tf = FFN_TM, FFN_TF
    n_t, n_f = t // tm, D_FF // tf
    row = lambda i, f: (i, 0)
    fixed = lambda i, f: (0, 0)
    once = pl.Buffered(1)
    vec = pl.BlockSpec((1, D_MODEL), fixed, pipeline_mode=once)

    def square(lyr):
        return pl.BlockSpec((None, D_MODEL, D_MODEL), lambda i, f: (lyr, 0, 0), pipeline_mode=once)

    in_specs = [
        pl.BlockSpec((tm, D_MODEL), row),
        vec,
        pl.BlockSpec((None, D_MODEL, tf), lambda i, f: (layer, 0, f)),
        pl.BlockSpec((None, D_MODEL, tf), lambda i, f: (layer, 0, f + n_f)),
        pl.BlockSpec((None, tf, D_MODEL), lambda i, f: (layer, f, 0)),
    ]
    args = [h, norm_g.reshape(1, D_MODEL), w_in, w_in, w_out]
    if attn_in is not None:
        o, proj_layer, w_proj = attn_in
        in_specs += [pl.BlockSpec((tm, V_DIM), row), square(proj_layer)]
        args += [o, w_proj]
    if ple is not None:
        pn, ple_layer, pwg, p, pwp = ple
        in_specs += [
            vec,
            square(ple_layer),
            pl.BlockSpec((None, tm, PLE_DIM), lambda i, f: (ple_layer, i, 0)),
            pl.BlockSpec((None, PLE_DIM, D_MODEL), lambda i, f: (ple_layer, 0, 0),
                         pipeline_mode=once),
        ]
        args += [pn.reshape(1, D_MODEL), pwg, p, pwp]
    if final_g is not None:
        in_specs.append(vec)
        args.append(final_g.reshape(1, D_MODEL))
    if q_out is not None:
        g_q, q_layer, w_q = q_out
        in_specs += [vec, square(q_layer)]
        args += [g_q.reshape(1, D_MODEL), w_q]
    out_specs = [pl.BlockSpec((tm, D_MODEL), row)]
    out_shape = [jax.ShapeDtypeStruct((t, D_MODEL), F32)]
    if q_out is not None:
        out_specs.append(pl.BlockSpec((tm, QK_DIM), row))
        out_shape.append(jax.ShapeDtypeStruct((t, QK_DIM), BF16))
    if cast_next is not None:
        nw_in, nw_out, n_layer = cast_next
        steps = n_t * n_f
        r_in, r_out = D_MODEL // steps, D_FF // steps
        slab = lambda i, f: (i * n_f + f, 0)
        in_specs += [
            pl.BlockSpec((None, r_in, 2 * D_FF), lambda i, f: (n_layer, i * n_f + f, 0)),
            pl.BlockSpec((None, r_out, D_MODEL), lambda i, f: (n_layer, i * n_f + f, 0)),
        ]
        args += [nw_in, nw_out]
        out_specs += [pl.BlockSpec((r_in, 2 * D_FF), slab), pl.BlockSpec((r_out, D_MODEL), slab)]
        out_shape += [jax.ShapeDtypeStruct((D_MODEL, 2 * D_FF), BF16),
                      jax.ShapeDtypeStruct((D_FF, D_MODEL), BF16)]
    body = functools.partial(_ffn_body, n_f=n_f, attn_in=attn_in is not None, ple=ple is not None,
                             final=final_g is not None, q_out=q_out is not None,
                             cast=cast_next is not None)
    return pl.pallas_call(
        body,
        grid=(n_t, n_f),
        in_specs=in_specs,
        out_specs=out_specs,
        out_shape=out_shape,
        scratch_shapes=[pltpu.VMEM((tm, D_MODEL), BF16)],
        compiler_params=_params(("parallel", "arbitrary")),
        name="ffn",
    )(*args)


def _conv_body(h_ref, g_ref, win_ref, bin_ref, wdw_ref, bdw_ref, lng_ref, lnb_ref,
               wout_ref, bout_ref, out_ref, ubuf_ref, cbuf_ref, sh_ref, *, ts):
    n_lb = D_MODEL // LANES
    halo = CONV_HALO
    first_tap_row = halo - (CONV_WIDTH - 1)

    @pl.when(pl.program_id(1) == 0)
    def _():
        ubuf_ref[:, 0:halo, :] = jnp.zeros((n_lb, halo, LANES), F32)

    x = h_ref[...]
    xn = _rms(x, g_ref[...]).astype(BF16)
    y = jnp.dot(xn, win_ref[...], preferred_element_type=F32) + bin_ref[...]
    u = y[:, :D_MODEL] * _sigmoid(y[:, D_MODEL:])
    for j in range(n_lb):
        ubuf_ref[j, halo:halo + ts, :] = u[:, j * LANES:(j + 1) * LANES]

    def lane_block(j, carry):
        for r in range(PACK):
            n = ts + (PACK if r < PACK - 1 else 0)
            sh_ref[r, 0:n, :] = ubuf_ref[j, pl.ds(first_tap_row + r, n), :].astype(BF16)
        for g in range(ts // PACK):
            acc = None
            for k in range(CONV_WIDTH):
                a, r = divmod(k, PACK)
                seg = sh_ref[r, (g + a) * PACK:(g + a + 1) * PACK, :]
                term = seg.astype(F32) * wdw_ref[j, k].astype(F32)
                acc = term if acc is None else acc + term
            cbuf_ref[j, g * PACK:(g + 1) * PACK, :] = acc
        return carry

    lax.fori_loop(0, n_lb, lane_block, 0)

    ubuf_ref[:, 0:halo, :] = ubuf_ref[:, ts:ts + halo, :]

    c = jnp.concatenate([cbuf_ref[j] for j in range(n_lb)], axis=1) + bdw_ref[...]
    mu = jnp.mean(c, axis=-1, keepdims=True)
    d = c - mu
    var = jnp.mean(d * d, axis=-1, keepdims=True)
    z = d * lax.rsqrt(var + LN_EPS) * lng_ref[...] + lnb_ref[...]
    z = (z * _sigmoid(z)).astype(BF16)
    out_ref[...] = x + jnp.dot(z, wout_ref[...], preferred_element_type=F32) + bout_ref[...]


def _conv_layer(h, batch, seq, layer, norm_g, w_in, b_in, w_dw, b_dw, ln_g, ln_b, w_out, b_out):
    ts = CONV_TS
    n_s = seq // ts
    n_lb = D_MODEL // LANES
    wdw = jnp.broadcast_to(
        w_dw.astype(BF16).reshape(CONV_WIDTH, n_lb, 1, LANES).transpose(1, 0, 2, 3),
        (n_lb, CONV_WIDTH, PACK, LANES))
    row = lambda b, s: (b * n_s + s, 0)
    fixed = lambda b, s: (0, 0)
    vec = pl.BlockSpec((1, D_MODEL), fixed)
    return pl.pallas_call(
        functools.partial(_conv_body, ts=ts),
        grid=(batch, n_s),
        in_specs=[
            pl.BlockSpec((ts, D_MODEL), row),
            vec,
            pl.BlockSpec((None, D_MODEL, 2 * D_MODEL), lambda b, s: (layer, 0, 0)),
            pl.BlockSpec((1, 2 * D_MODEL), fixed),
            pl.BlockSpec((n_lb, CONV_WIDTH, PACK, LANES), lambda b, s: (0, 0, 0, 0)),
            vec, vec, vec,
            pl.BlockSpec((None, D_MODEL, D_MODEL), lambda b, s: (layer, 0, 0)),
            vec,
        ],
        out_specs=pl.BlockSpec((ts, D_MODEL), row),
        out_shape=jax.ShapeDtypeStruct(h.shape, F32),
        scratch_shapes=[
            pltpu.VMEM((n_lb, CONV_HALO + ts, LANES), F32),
            pltpu.VMEM((n_lb, ts, LANES), F32),
            pltpu.VMEM((PACK, ts + PACK, LANES), BF16),
        ],
        compiler_params=_params(("parallel", "arbitrary")),
        name="conformer_conv",
    )(h, norm_g.reshape(1, -1), w_in, b_in.reshape(1, -1), wdw, b_dw.reshape(1, -1),
      ln_g.reshape(1, -1), ln_b.reshape(1, -1), w_out, b_out.reshape(1, -1))


def _bucket_tiles(t):
    assert t >= MAX_DISTANCE
    qp = np.arange(t)[:, None]
    kp = np.arange(t)[None, :]
    max_exact = NUM_BUCKETS // 2

    def bucket(n):
        nf = np.maximum(n, 1).astype(np.float32)
        large = max_exact + (np.log(nf / max_exact) / math.log(MAX_DISTANCE / max_exact)
                             * (NUM_BUCKETS - max_exact)).astype(np.int32)
        return np.where(n < max_exact, n, np.minimum(large, NUM_BUCKETS - 1))

    diag = np.where(kp <= qp, bucket(np.maximum(qp - kp, 0)), -1)
    left = bucket(qp + t - kp)
    return np.stack([diag, left]).astype(np.int32)


def _bias_body(rb_ref, bucket_ref, out_ref):
    h = pl.program_id(0)
    far = rb_ref[NUM_BUCKETS - 1, h]
    bk = bucket_ref[...]
    acc = jnp.full(bk.shape, NEG_INF, F32)
    for b in range(NUM_BUCKETS):
        acc = jnp.where(bk == b, rb_ref[b, h] - far, acc)
    out_ref[...] = acc


def _rel_bias_tiles(rel_bias, t):
    return pl.pallas_call(
        _bias_body,
        grid=(N_HEADS,),
        in_specs=[
            pl.BlockSpec(memory_space=pltpu.SMEM),
            pl.BlockSpec((2, t, t), lambda h: (0, 0, 0)),
        ],
        out_specs=pl.BlockSpec((None, 2, t, t), lambda h: (h, 0, 0, 0)),
        out_shape=jax.ShapeDtypeStruct((N_HEADS, 2, t, t), F32),
        compiler_params=_params(("parallel",)),
        name="rel_bias_tiles",
    )(rel_bias.astype(F32), jnp.asarray(_bucket_tiles(t)))


def _attn_body(q_ref, kt_ref, v_ref, bias_ref, lam_ref, sub_ref, out_ref, vaug_ref,
               *, t, n_q, lambda_init):
    seq = v_ref.shape[0]
    lq = lam_ref[...]
    lam = (jnp.exp(jnp.sum(lq[0:1] * lq[1:2], axis=-1, keepdims=True))
           - jnp.exp(jnp.sum(lq[2:3] * lq[3:4], axis=-1, keepdims=True)) + lambda_init)
    for hh in range(ATT_HEADS):
        vaug_ref[hh, :, :LANES] = v_ref[:, hh * LANES:(hh + 1) * LANES]
        vaug_ref[hh, :, LANES:] = jnp.ones((seq, LANES), BF16)

    for hh, i in [(hh, i) for hh in range(ATT_HEADS) for i in reversed(range(n_q))]:
        head = slice(hh * LANES, (hh + 1) * LANES)
        q = q_ref[i * t:(i + 1) * t, head]
        lane = lax.broadcasted_iota(jnp.int32, q.shape, 1)
        zero = jnp.zeros_like(q)
        q12 = jnp.concatenate([jnp.where(lane < HEAD_DIM, q, zero),
                               jnp.where(lane >= HEAD_DIM, q, zero)], axis=0)
        pieces = []
        if i >= 2:
            pieces.append((0, (i - 1) * t, None))
        if i >= 1:
            pieces.append(((i - 1) * t, i * t, 1))
        pieces.append((i * t, (i + 1) * t, 0))

        scores = []
        colmax = None
        for c0, c1, bias_idx in pieces:
            s = jnp.dot(q12, kt_ref[head, c0:c1], preferred_element_type=F32)
            if bias_idx is not None:
                s = (s.reshape(2, t, t) + bias_ref[hh, bias_idx][None]).reshape(2 * t, t)
            for c in range((c1 - c0) // LANES):
                blk = s[:, c * LANES:(c + 1) * LANES]
                colmax = blk if colmax is None else jnp.maximum(colmax, blk)
            scores.append(s)
        m = jnp.max(colmax, axis=1, keepdims=True)

        acc = None
        for (c0, c1, _), s in zip(pieces, scores):
            p = jnp.exp(s - m).astype(BF16)
            d = jnp.dot(p, vaug_ref[hh, c0:c1, :], preferred_element_type=F32)
            acc = d if acc is None else acc + d
        o12 = acc[:, :LANES] / acc[:, LANES:]
        o = o12[:t] - lam * o12[t:]
        out_ref[i * t:(i + 1) * t, head] = (
            _rms(o, sub_ref[...]) * (1.0 - lambda_init)).astype(out_ref.dtype)


def _attention(q, kt, v, bias_tiles, lam_rows, subln, batch, seq, lambda_init):
    t = ATT_T
    n_q = seq // t
    q3 = q.reshape(batch, seq, QK_DIM)
    width = ATT_HEADS * LANES
    head_cols = pl.BlockSpec((None, seq, width), lambda b, h: (b, 0, h))
    out = pl.pallas_call(
        functools.partial(_attn_body, t=t, n_q=n_q, lambda_init=lambda_init),
        grid=(batch, N_HEADS // ATT_HEADS),
        in_specs=[
            head_cols,
            pl.BlockSpec((None, width, seq), lambda b, h: (b, h, 0)),
            head_cols,
            pl.BlockSpec((ATT_HEADS, 2, t, t), lambda b, h: (h, 0, 0, 0)),
            pl.BlockSpec((4, HEAD_DIM), lambda b, h: (0, 0)),
            pl.BlockSpec((1, LANES), lambda b, h: (0, 0)),
        ],
        out_specs=head_cols,
        out_shape=jax.ShapeDtypeStruct((batch, seq, V_DIM), BF16),
        scratch_shapes=[pltpu.VMEM((ATT_HEADS, seq, 2 * LANES), BF16)],
        compiler_params=_params(("parallel", "parallel")),
        name="diff_attention",
    )(q3, kt, v, bias_tiles, lam_rows, subln.reshape(1, LANES))
    return out.reshape(batch * seq, V_DIM)


def _kv_body(h_ref, g_ref, wkt_ref, wv_ref, kt_ref, v_ref):
    xn = _rms(h_ref[...], g_ref[...]).astype(BF16)
    kt = lax.dot_general(wkt_ref[...], xn, (((1,), (1,)), ((), ())),
                         preferred_element_type=F32)
    kt_ref[...] = kt.astype(BF16)
    v_ref[...] = jnp.dot(xn, wv_ref[...], preferred_element_type=F32).astype(BF16)


def _shared_kv(h, batch, seq, norm_g, w_kt, w_v):
    tm = PROJ_TM
    n_s = seq // tm
    fixed = lambda b, s: (0, 0)
    return pl.pallas_call(
        _kv_body,
        grid=(batch, n_s),
        in_specs=[
            pl.BlockSpec((tm, D_MODEL), lambda b, s: (b * n_s + s, 0)),
            pl.BlockSpec((1, D_MODEL), fixed),
            pl.BlockSpec((QK_DIM, D_MODEL), fixed),
            pl.BlockSpec((D_MODEL, V_DIM), fixed),
        ],
        out_specs=[
            pl.BlockSpec((None, QK_DIM, tm), lambda b, s: (b, 0, s)),
            pl.BlockSpec((None, tm, V_DIM), lambda b, s: (b, s, 0)),
        ],
        out_shape=[
            jax.ShapeDtypeStruct((batch, QK_DIM, seq), BF16),
            jax.ShapeDtypeStruct((batch, seq, V_DIM), BF16),
        ],
        compiler_params=_params(("parallel", "parallel")),
        name="shared_kv",
    )(h, norm_g.reshape(1, D_MODEL), w_kt, w_v)


def _head_major(w):
    lead = w.shape[:-1]
    w = w.reshape(*lead, 2, N_HEADS, HEAD_DIM)
    return jnp.swapaxes(w, -3, -2).reshape(*lead, QK_DIM)


def kernel(x, p, ffn1_norm, ffn1_w_in, ffn1_w_out, mix_norm, ffn2_norm, ffn2_w_in, ffn2_w_out,
           ple_norm, ple_w_gate, ple_w_proj, conv_w_in, conv_b_in, conv_w_dw, conv_b_dw,
           conv_ln_g, conv_ln_b, conv_w_out, conv_b_out, kv_norm, w_kv, attn_w_q,
           attn_lq1, attn_lk1, attn_lq2, attn_lk2, attn_subln, attn_w_o, rel_bias, final_norm):
    batch, seq, _ = x.shape
    tokens = batch * seq
    h = x.reshape(tokens, D_MODEL)
    p2 = p.reshape(DEPTH, tokens, PLE_DIM)
    bias_tiles = _rel_bias_tiles(rel_bias, ATT_T)
    w_kt = _head_major(w_kv[:, :QK_DIM]).T.astype(BF16)
    w_v = w_kv[:, QK_DIM:].astype(BF16)
    ple_w_gate, ple_w_proj = ple_w_gate.astype(BF16), ple_w_proj.astype(BF16)
    conv_w_in, conv_w_out = conv_w_in.astype(BF16), conv_w_out.astype(BF16)
    attn_w_o = attn_w_o.astype(BF16)
    w_q = (_head_major(attn_w_q) * (HEAD_DIM ** -0.5)).astype(BF16)
    w_in, w_out = ffn1_w_in[0:1].astype(BF16), ffn1_w_out[0:1].astype(BF16)
    kt = v = o = None
    for i in range(DEPTH):
        j = i - N_A
        if i == N_A:
            kt, v = _shared_kv(h, batch, seq, kv_norm, w_kt, w_v)
        res = _ffn(h, 0, ffn1_norm[i], w_in, w_out,
                   q_out=(mix_norm[i], j, w_q) if i >= N_A else None,
                   cast_next=(ffn2_w_in, ffn2_w_out, i))
        h, w_in, w_out = res[0], res[-2][None], res[-1][None]
        if i < N_A:
            h = _conv_layer(h, batch, seq, i, mix_norm[i], conv_w_in, conv_b_in[i],
                            conv_w_dw[i], conv_b_dw[i], conv_ln_g[i], conv_ln_b[i],
                            conv_w_out, conv_b_out[i])
        else:
            lambda_init = 0.8 - 0.6 * math.exp(-0.3 * i)
            lam_rows = jnp.stack([attn_lq1[j], attn_lk1[j], attn_lq2[j], attn_lk2[j]]).astype(F32)
            o = _attention(res[1], kt, v, bias_tiles, lam_rows, attn_subln[j], batch, seq,
                           lambda_init)
        last = i == DEPTH - 1
        res = _ffn(h, 0, ffn2_norm[i], w_in, w_out,
                   attn_in=(o, j, attn_w_o) if i >= N_A else None,
                   ple=(ple_norm[i], i, ple_w_gate, p2, ple_w_proj),
                   final_g=final_norm if last else None,
                   cast_next=None if last else (ffn1_w_in, ffn1_w_out, i + 1))
        h = res[0]
        if not last:
            w_in, w_out = res[-2][None], res[-1][None]
    return h.reshape(batch, seq, D_MODEL)
```

```python
import functools
import math

import jax
import jax.numpy as jnp
import numpy as np
from jax import lax
from jax.experimental import pallas as pl
from jax.experimental.pallas import tpu as pltpu

D_MODEL = 1024
DEPTH = 4
N_A = DEPTH // 2
D_FF = 4 * D_MODEL
CONV_WIDTH = 31
HEAD_DIM = 64
N_HEADS = D_MODEL // (2 * HEAD_DIM)
QK_DIM = 2 * N_HEADS * HEAD_DIM
V_DIM = N_HEADS * 2 * HEAD_DIM
NUM_BUCKETS = 32
MAX_DISTANCE = 128
PLE_DIM = 256
RMS_EPS = 1e-6
LN_EPS = 1e-5
NEG_INF = -1e30

LANES = 128
PACK = 16
VMEM_LIMIT_BYTES = 56 * 1024 * 1024

FFN_TM = 1024
FFN_TF = 1024
FFN_SUB = 256
PROJ_TM = 1024
CONV_TS = 512
CONV_HALO = 32
ATT_T = 256
ATT_HEADS = 2

F32 = jnp.float32
BF16 = jnp.bfloat16


def _rms(x, g):
    ms = jnp.mean(x * x, axis=-1, keepdims=True)
    return x * lax.rsqrt(ms + RMS_EPS) * g


def _sigmoid(x):
    return 1.0 / (1.0 + jnp.exp(-x))


def _params(sem):
    return pltpu.CompilerParams(dimension_semantics=sem, vmem_limit_bytes=VMEM_LIMIT_BYTES)


def _ffn_body(*refs, n_f, attn_in, ple, final, q_out, cast):
    it = iter(refs)
    h_ref, g_ref, wg_ref, wu_ref, wo_ref = (next(it) for _ in range(5))
    if attn_in:
        o_ref, wproj_ref = next(it), next(it)
    if ple:
        pn_ref, pwg_ref, p_ref, pwp_ref = (next(it) for _ in range(4))
    if final:
        fn_ref = next(it)
    if q_out:
        qg_ref, wq_ref = next(it), next(it)
    if cast:
        nwin_ref, nwout_ref = next(it), next(it)
    out_ref = next(it)
    if q_out:
        q_ref = next(it)
    if cast:
        cwin_ref, cwout_ref = next(it), next(it)
    xn_ref = next(it)

    f = pl.program_id(1)

    if cast:
        cwin_ref[...] = nwin_ref[...].astype(BF16)
        cwout_ref[...] = nwout_ref[...].astype(BF16)

    def step(first):
        if first:
            x = h_ref[...]
            if attn_in:
                x = x + jnp.dot(o_ref[...], wproj_ref[...], preferred_element_type=F32)
                out_ref[...] = x
            xn = _rms(x, g_ref[...]).astype(BF16)
            xn_ref[...] = xn
        else:
            xn = xn_ref[...]
        part = None
        for c in range(FFN_TF // FFN_SUB):
            cols = slice(c * FFN_SUB, (c + 1) * FFN_SUB)
            gate = jnp.dot(xn, wg_ref[:, cols], preferred_element_type=F32)
            up = jnp.dot(xn, wu_ref[:, cols], preferred_element_type=F32)
            act = (gate * _sigmoid(gate) * up).astype(BF16)
            d = jnp.dot(act, wo_ref[cols, :], preferred_element_type=F32)
            part = d if part is None else part + d
        if first and not attn_in:
            out_ref[...] = h_ref[...] + 0.5 * part
        else:
            out_ref[...] += 0.5 * part

    pl.when(f == 0)(functools.partial(step, True))
    pl.when(f > 0)(functools.partial(step, False))

    @pl.when(f == n_f - 1)
    def _():
        hn = out_ref[...]
        if ple:
            xg = _rms(hn, pn_ref[...]).astype(BF16)
            gt = _sigmoid(jnp.dot(xg, pwg_ref[...], preferred_element_type=F32))
            pr = jnp.dot(p_ref[...].astype(BF16), pwp_ref[...], preferred_element_type=F32)
            hn = hn + gt * pr
        if final:
            hn = _rms(hn, fn_ref[...])
        out_ref[...] = hn
        if q_out:
            xq = _rms(hn, qg_ref[...]).astype(BF16)
            q_ref[...] = jnp.dot(xq, wq_ref[...], preferred_element_type=F32).astype(BF16)


def _ffn(h, layer, norm_g, w_in, w_out, attn_in=None, ple=None, final_g=None, q_out=None,
         cast_next=None):
    t = h.shape[0]
    tm, tf = FFN_TM, FFN_TF
    n_t, n_f = t // tm, D_FF // tf
    row = lambda i, f: (i, 0)
    fixed = lambda i, f: (0, 0)
    once = pl.Buffered(1)
    vec = pl.BlockSpec((1, D_MODEL), fixed, pipeline_mode=once)

    def square(lyr):
        return pl.BlockSpec((None, D_MODEL, D_MODEL), lambda i, f: (lyr, 0, 0), pipeline_mode=once)

    in_specs = [
        pl.BlockSpec((tm, D_MODEL), row),
        vec,
        pl.BlockSpec((None, D_MODEL, tf), lambda i, f: (layer, 0, f)),
        pl.BlockSpec((None, D_MODEL, tf), lambda i, f: (layer, 0, f + n_f)),
        pl.BlockSpec((None, tf, D_MODEL), lambda i, f: (layer, f, 0)),
    ]
    args = [h, norm_g.reshape(1, D_MODEL), w_in, w_in, w_out]
    if attn_in is not None:
        o, proj_layer, w_proj = attn_in
        in_specs += [pl.BlockSpec((tm, V_DIM), row), square(proj_layer)]
        args += [o, w_proj]
    if ple is not None:
        pn, ple_layer, pwg, p, pwp = ple
        in_specs += [
            vec,
            square(ple_layer),
            pl.BlockSpec((None, tm, PLE_DIM), lambda i, f: (ple_layer, i, 0)),
            pl.BlockSpec((None, PLE_DIM, D_MODEL), lambda i, f: (ple_layer, 0, 0),
                         pipeline_mode=once),
        ]
        args += [pn.reshape(1, D_MODEL), pwg, p, pwp]
    if final_g is not None:
        in_specs.append(vec)
        args.append(final_g.reshape(1, D_MODEL))
    if q_out is not None:
        g_q, q_layer, w_q = q_out
        in_specs += [vec, square(q_layer)]
        args += [g_q.reshape(1, D_MODEL), w_q]
    out_specs = [pl.BlockSpec((tm, D_MODEL), row)]
    out_shape = [jax.ShapeDtypeStruct((t, D_MODEL), F32)]
    if q_out is not None:
        out_specs.append(pl.BlockSpec((tm, QK_DIM), row))
        out_shape.append(jax.ShapeDtypeStruct((t, QK_DIM), BF16))
    if cast_next is not None:
        nw_in, nw_out, n_layer = cast_next
        steps = n_t * n_f
        r_in, r_out = D_MODEL // steps, D_FF // steps
        slab = lambda i, f: (i * n_f + f, 0)
        in_specs += [
            pl.BlockSpec((None, r_in, 2 * D_FF), lambda i, f: (n_layer, i * n_f + f, 0)),
            pl.BlockSpec((None, r_out, D_MODEL), lambda i, f: (n_layer, i * n_f + f, 0)),
        ]
        args += [nw_in, nw_out]
        out_specs += [pl.BlockSpec((r_in, 2 * D_FF), slab), pl.BlockSpec((r_out, D_MODEL), slab)]
        out_shape += [jax.ShapeDtypeStruct((D_MODEL, 2 * D_FF), BF16),
                      jax.ShapeDtypeStruct((D_FF, D_MODEL), BF16)]
    body = functools.partial(_ffn_body, n_f=n_f, attn_in=attn_in is not None, ple=ple is not None,
                             final=final_g is not None, q_out=q_out is not None,
                             cast=cast_next is not None)
    return pl.pallas_call(
        body,
        grid=(n_t, n_f),
        in_specs=in_specs,
        out_specs=out_specs,
        out_shape=out_shape,
        scratch_shapes=[pltpu.VMEM((tm, D_MODEL), BF16)],
        compiler_params=_params(("parallel", "arbitrary")),
        name="ffn",
    )(*args)


def _conv_body(h_ref, g_ref, win_ref, bin_ref, wdw_ref, bdw_ref, lng_ref, lnb_ref,
               wout_ref, bout_ref, out_ref, ubuf_ref, cbuf_ref, sh_ref, *, ts):
    n_lb = D_MODEL // LANES
    halo = CONV_HALO
    first_tap_row = halo - (CONV_WIDTH - 1)

    @pl.when(pl.program_id(1) == 0)
    def _():
        ubuf_ref[:, 0:halo, :] = jnp.zeros((n_lb, halo, LANES), F32)

    x = h_ref[...]
    xn = _rms(x, g_ref[...]).astype(BF16)
    y = jnp.dot(xn, win_ref[...], preferred_element_type=F32) + bin_ref[...]
    u = y[:, :D_MODEL] * _sigmoid(y[:, D_MODEL:])
    for j in range(n_lb):
        ubuf_ref[j, halo:halo + ts, :] = u[:, j * LANES:(j + 1) * LANES]

    def lane_block(j, carry):
        for r in range(PACK):
            n = ts + (PACK if r < PACK - 1 else 0)
            sh_ref[r, 0:n, :] = ubuf_ref[j, pl.ds(first_tap_row + r, n), :].astype(BF16)
        for g in range(ts // PACK):
            acc = None
            for k in range(CONV_WIDTH):
                a, r = divmod(k, PACK)
                seg = sh_ref[r, (g + a) * PACK:(g + a + 1) * PACK, :]
                term = seg.astype(F32) * wdw_ref[j, k].astype(F32)
                acc = term if acc is None else acc + term
            cbuf_ref[j, g * PACK:(g + 1) * PACK, :] = acc
        return carry

    lax.fori_loop(0, n_lb, lane_block, 0)

    ubuf_ref[:, 0:halo, :] = ubuf_ref[:, ts:ts + halo, :]

    c = jnp.concatenate([cbuf_ref[j] for j in range(n_lb)], axis=1) + bdw_ref[...]
    mu = jnp.mean(c, axis=-1, keepdims=True)
    d = c - mu
    var = jnp.mean(d * d, axis=-1, keepdims=True)
    z = d * lax.rsqrt(var + LN_EPS) * lng_ref[...] + lnb_ref[...]
    z = (z * _sigmoid(z)).astype(BF16)
    out_ref[...] = x + jnp.dot(z, wout_ref[...], preferred_element_type=F32) + bout_ref[...]


def _conv_layer(h, batch, seq, layer, norm_g, w_in, b_in, w_dw, b_dw, ln_g, ln_b, w_out, b_out):
    ts = CONV_TS
    n_s = seq // ts
    n_lb = D_MODEL // LANES
    wdw = jnp.broadcast_to(
        w_dw.astype(BF16).reshape(CONV_WIDTH, n_lb, 1, LANES).transpose(1, 0, 2, 3),
        (n_lb, CONV_WIDTH, PACK, LANES))
    row = lambda b, s: (b * n_s + s, 0)
    fixed = lambda b, s: (0, 0)
    vec = pl.BlockSpec((1, D_MODEL), fixed)
    return pl.pallas_call(
        functools.partial(_conv_body, ts=ts),
        grid=(batch, n_s),
        in_specs=[
            pl.BlockSpec((ts, D_MODEL), row),
            vec,
            pl.BlockSpec((None, D_MODEL, 2 * D_MODEL), lambda b, s: (layer, 0, 0)),
            pl.BlockSpec((1, 2 * D_MODEL), fixed),
            pl.BlockSpec((n_lb, CONV_WIDTH, PACK, LANES), lambda b, s: (0, 0, 0, 0)),
            vec, vec, vec,
            pl.BlockSpec((None, D_MODEL, D_MODEL), lambda b, s: (layer, 0, 0)),
            vec,
        ],
        out_specs=pl.BlockSpec((ts, D_MODEL), row),
        out_shape=jax.ShapeDtypeStruct(h.shape, F32),
        scratch_shapes=[
            pltpu.VMEM((n_lb, CONV_HALO + ts, LANES), F32),
            pltpu.VMEM((n_lb, ts, LANES), F32),
            pltpu.VMEM((PACK, ts + PACK, LANES), BF16),
        ],
        compiler_params=_params(("parallel", "arbitrary")),
        name="conformer_conv",
    )(h, norm_g.reshape(1, -1), w_in, b_in.reshape(1, -1), wdw, b_dw.reshape(1, -1),
      ln_g.reshape(1, -1), ln_b.reshape(1, -1), w_out, b_out.reshape(1, -1))


def _bucket_tiles(t):
    assert t >= MAX_DISTANCE
    qp = np.arange(t)[:, None]
    kp = np.arange(t)[None, :]
    max_exact = NUM_BUCKETS // 2

    def bucket(n):
        nf = np.maximum(n, 1).astype(np.float32)
        large = max_exact + (np.log(nf / max_exact) / math.log(MAX_DISTANCE / max_exact)
                             * (NUM_BUCKETS - max_exact)).astype(np.int32)
        return np.where(n < max_exact, n, np.minimum(large, NUM_BUCKETS - 1))

    diag = np.where(kp <= qp, bucket(np.maximum(qp - kp, 0)), -1)
    left = bucket(qp + t - kp)
    return np.stack([diag, left]).astype(np.int32)


def _bias_body(rb_ref, bucket_ref, out_ref):
    h = pl.program_id(0)
    far = rb_ref[NUM_BUCKETS - 1, h]
    bk = bucket_ref[...]
    acc = jnp.full(bk.shape, NEG_INF, F32)
    for b in range(NUM_BUCKETS):
        acc = jnp.where(bk == b, rb_ref[b, h] - far, acc)
    out_ref[...] = acc


def _rel_bias_tiles(rel_bias, t):
    return pl.pallas_call(
        _bias_body,
        grid=(N_HEADS,),
        in_specs=[
            pl.BlockSpec(memory_space=pltpu.SMEM),
            pl.BlockSpec((2, t, t), lambda h: (0, 0, 0)),
        ],
        out_specs=pl.BlockSpec((None, 2, t, t), lambda h: (h, 0, 0, 0)),
        out_shape=jax.ShapeDtypeStruct((N_HEADS, 2, t, t), F32),
        compiler_params=_params(("parallel",)),
        name="rel_bias_tiles",
    )(rel_bias.astype(F32), jnp.asarray(_bucket_tiles(t)))


def _attn_body(q_ref, kt_ref, v_ref, bias_ref, lam_ref, sub_ref, out_ref, vaug_ref,
               *, t, n_q, lambda_init):
    seq = v_ref.shape[0]
    lq = lam_ref[...]
    lam = (jnp.exp(jnp.sum(lq[0:1] * lq[1:2], axis=-1, keepdims=True))
           - jnp.exp(jnp.sum(lq[2:3] * lq[3:4], axis=-1, keepdims=True)) + lambda_init)
    for hh in range(ATT_HEADS):
        vaug_ref[hh, :, :LANES] = v_ref[:, hh * LANES:(hh + 1) * LANES]
        vaug_ref[hh, :, LANES:] = jnp.ones((seq, LANES), BF16)

    for hh, i in [(hh, i) for hh in range(ATT_HEADS) for i in reversed(range(n_q))]:
        head = slice(hh * LANES, (hh + 1) * LANES)
        q = q_ref[i * t:(i + 1) * t, head]
        lane = lax.broadcasted_iota(jnp.int32, q.shape, 1)
        zero = jnp.zeros_like(q)
        q12 = jnp.concatenate([jnp.where(lane < HEAD_DIM, q, zero),
                               jnp.where(lane >= HEAD_DIM, q, zero)], axis=0)
        pieces = []
        if i >= 2:
            pieces.append((0, (i - 1) * t, None))
        if i >= 1:
            pieces.append(((i - 1) * t, i * t, 1))
        pieces.append((i * t, (i + 1) * t, 0))

        scores = []
        colmax = None
        for c0, c1, bias_idx in pieces:
            s = jnp.dot(q12, kt_ref[head, c0:c1], preferred_element_type=F32)
            if bias_idx is not None:
                s = (s.reshape(2, t, t) + bias_ref[hh, bias_idx][None]).reshape(2 * t, t)
            for c in range((c1 - c0) // LANES):
                blk = s[:, c * LANES:(c + 1) * LANES]
                colmax = blk if colmax is None else jnp.maximum(colmax, blk)
            scores.append(s)
        m = jnp.max(colmax, axis=1, keepdims=True)

        acc = None
        for (c0, c1, _), s in zip(pieces, scores):
            p = jnp.exp(s - m).astype(BF16)
            d = jnp.dot(p, vaug_ref[hh, c0:c1, :], preferred_element_type=F32)
            acc = d if acc is None else acc + d
        o12 = acc[:, :LANES] / acc[:, LANES:]
        o = o12[:t] - lam * o12[t:]
        out_ref[i * t:(i + 1) * t, head] = (
            _rms(o, sub_ref[...]) * (1.0 - lambda_init)).astype(out_ref.dtype)


def _attention(q, kt, v, bias_tiles, lam_rows, subln, batch, seq, lambda_init):
    t = ATT_T
    n_q = seq // t
    q3 = q.reshape(batch, seq, QK_DIM)
    width = ATT_HEADS * LANES
    head_cols = pl.BlockSpec((None, seq, width), lambda b, h: (b, 0, h))
    out = pl.pallas_call(
        functools.partial(_attn_body, t=t, n_q=n_q, lambda_init=lambda_init),
        grid=(batch, N_HEADS // ATT_HEADS),
        in_specs=[
            head_cols,
            pl.BlockSpec((None, width, seq), lambda b, h: (b, h, 0)),
            head_cols,
            pl.BlockSpec((ATT_HEADS, 2, t, t), lambda b, h: (h, 0, 0, 0)),
            pl.BlockSpec((4, HEAD_DIM), lambda b, h: (0, 0)),
            pl.BlockSpec((1, LANES), lambda b, h: (0, 0)),
        ],
        out_specs=head_cols,
        out_shape=jax.ShapeDtypeStruct((batch, seq, V_DIM), BF16),
        scratch_shapes=[pltpu.VMEM((ATT_HEADS, seq, 2 * LANES), BF16)],
        compiler_params=_params(("parallel", "parallel")),
        name="diff_attention",
    )(q3, kt, v, bias_tiles, lam_rows, subln.reshape(1, LANES))
    return out.reshape(batch * seq, V_DIM)


def _kv_body(h_ref, g_ref, wkt_ref, wv_ref, kt_ref, v_ref):
    xn = _rms(h_ref[...], g_ref[...]).astype(BF16)
    kt = lax.dot_general(wkt_ref[...], xn, (((1,), (1,)), ((), ())),
                         preferred_element_type=F32)
    kt_ref[...] = kt.astype(BF16)
    v_ref[...] = jnp.dot(xn, wv_ref[...], preferred_element_type=F32).astype(BF16)


def _shared_kv(h, batch, seq, norm_g, w_kt, w_v):
    tm = PROJ_TM
    n_s = seq // tm
    fixed = lambda b, s: (0, 0)
    return pl.pallas_call(
        _kv_body,
        grid=(batch, n_s),
        in_specs=[
            pl.BlockSpec((tm, D_MODEL), lambda b, s: (b * n_s + s, 0)),
            pl.BlockSpec((1, D_MODEL), fixed),
            pl.BlockSpec((QK_DIM, D_MODEL), fixed),
            pl.BlockSpec((D_MODEL, V_DIM), fixed),
        ],
        out_specs=[
            pl.BlockSpec((None, QK_DIM, tm), lambda b, s: (b, 0, s)),
            pl.BlockSpec((None, tm, V_DIM), lambda b, s: (b, s, 0)),
        ],
        out_shape=[
            jax.ShapeDtypeStruct((batch, QK_DIM, seq), BF16),
            jax.ShapeDtypeStruct((batch, seq, V_DIM), BF16),
        ],
        compiler_params=_params(("parallel", "parallel")),
        name="shared_kv",
    )(h, norm_g.reshape(1, D_MODEL), w_kt, w_v)


def _head_major(w):
    lead = w.shape[:-1]
    w = w.reshape(*lead, 2, N_HEADS, HEAD_DIM)
    return jnp.swapaxes(w, -3, -2).reshape(*lead, QK_DIM)


def kernel(x, p, ffn1_norm, ffn1_w_in, ffn1_w_out, mix_norm, ffn2_norm, ffn2_w_in, ffn2_w_out,
           ple_norm, ple_w_gate, ple_w_proj, conv_w_in, conv_b_in, conv_w_dw, conv_b_dw,
           conv_ln_g, conv_ln_b, conv_w_out, conv_b_out, kv_norm, w_kv, attn_w_q,
           attn_lq1, attn_lk1, attn_lq2, attn_lk2, attn_subln, attn_w_o, rel_bias, final_norm):
    batch, seq, _ = x.shape
    tokens = batch * seq
    h = x.reshape(tokens, D_MODEL)
    p2 = p.reshape(DEPTH, tokens, PLE_DIM)
    bias_tiles = _rel_bias_tiles(rel_bias, ATT_T)
    w_kt = _head_major(w_kv[:, :QK_DIM]).T.astype(BF16)
    w_v = w_kv[:, QK_DIM:].astype(BF16)
    ple_w_gate, ple_w_proj = ple_w_gate.astype(BF16), ple_w_proj.astype(BF16)
    conv_w_in, conv_w_out = conv_w_in.astype(BF16), conv_w_out.astype(BF16)
    attn_w_o = attn_w_o.astype(BF16)
    w_q = (_head_major(attn_w_q) * (HEAD_DIM ** -0.5)).astype(BF16)
    w_in, w_out = ffn1_w_in[0:1].astype(BF16), ffn1_w_out[0:1].astype(BF16)
    kt = v = o = None
    for i in range(DEPTH):
        j = i - N_A
        if i == N_A:
            kt, v = _shared_kv(h, batch, seq, kv_norm, w_kt, w_v)
        res = _ffn(h, 0, ffn1_norm[i], w_in, w_out,
                   q_out=(mix_norm[i], j, w_q) if i >= N_A else None,
                   cast_next=(ffn2_w_in, ffn2_w_out, i))
        h, w_in, w_out = res[0], res[-2][None], res[-1][None]
        if i < N_A:
            h = _conv_layer(h, batch, seq, i, mix_norm[i], conv_w_in, conv_b_in[i],
                            conv_w_dw[i], conv_b_dw[i], conv_ln_g[i], conv_ln_b[i],
                            conv_w_out, conv_b_out[i])
        else:
            lambda_init = 0.8 - 0.6 * math.exp(-0.3 * i)
            lam_rows = jnp.stack([attn_lq1[j], attn_lk1[j], attn_lq2[j], attn_lk2[j]]).astype(F32)
            o = _attention(res[1], kt, v, bias_tiles, lam_rows, attn_subln[j], batch, seq,
                           lambda_init)
        last = i == DEPTH - 1
        res = _ffn(h, 0, ffn2_norm[i], w_in, w_out,
                   attn_in=(o, j, attn_w_o) if i >= N_A else None,
                   ple=(ple_norm[i], i, ple_w_gate, p2, ple_w_proj),
                   final_g=final_norm if last else None,
                   cast_next=None if last else (ffn1_w_in, ffn1_w_out, i + 1))
        h = res[0]
        if not last:
            w_in, w_out = res[-2][None], res[-1][None]
    return h.reshape(batch, seq, D_MODEL)
```

```python
import functools
import math

import jax
import jax.numpy as jnp
import numpy as np
from jax import lax
from jax.experimental import pallas as pl
from jax.experimental.pallas import tpu as pltpu

D_MODEL = 1024
DEPTH = 4
N_A = DEPTH // 2
D_FF = 4 * D_MODEL
CONV_WIDTH = 31
HEAD_DIM = 64
N_HEADS = D_MODEL // (2 * HEAD_DIM)
QK_DIM = 2 * N_HEADS * HEAD_DIM
V_DIM = N_HEADS * 2 * HEAD_DIM
NUM_BUCKETS = 32
MAX_DISTANCE = 128
PLE_DIM = 256
RMS_EPS = 1e-6
LN_EPS = 1e-5
NEG_INF = -1e30

LANES = 128
PACK = 16
VMEM_LIMIT_BYTES = 56 * 1024 * 1024

FFN_TM = 1024
FFN_TF = 1024
FFN_TF_BARE = 2048
FFN_SUB = 256
PROJ_TM = 1024
CONV_TS = 1024
CONV_HALO = 32
ATT_T = 256
ATT_HEADS = 4

F32 = jnp.float32
BF16 = jnp.bfloat16


def _rms(x, g):
    ms = jnp.mean(x * x, axis=-1, keepdims=True)
    return x * lax.rsqrt(ms + RMS_EPS) * g


def _sigmoid(x):
    return 1.0 / (1.0 + jnp.exp(-x))


def _params(sem):
    return pltpu.CompilerParams(dimension_semantics=sem, vmem_limit_bytes=VMEM_LIMIT_BYTES)


def _ffn_body(*refs, n_f, attn_in, ple, final, q_out, cast):
    it = iter(refs)
    h_ref, g_ref, wg_ref, wu_ref, wo_ref = (next(it) for _ in range(5))
    if attn_in:
        o_ref, wproj_ref = next(it), next(it)
    if ple:
        pn_ref, pwg_ref, p_ref, pwp_ref = (next(it) for _ in range(4))
    if final:
        fn_ref = next(it)
    if q_out:
        qg_ref, wq_ref = next(it), next(it)
    if cast:
        nwin_ref, nwout_ref = next(it), next(it)
    out_ref = next(it)
    if q_out:
        q_ref = next(it)
    if cast:
        cwin_ref, cwout_ref = next(it), next(it)
    xn_ref = next(it)

    f = pl.program_id(1)

    if cast:
        cwin_ref[...] = nwin_ref[...].astype(BF16)
        cwout_ref[...] = nwout_ref[...].astype(BF16)

    def step(first):
        if first:
            x = h_ref[...]
            if attn_in:
                x = x + jnp.dot(o_ref[...], wproj_ref[...], preferred_element_type=F32)
                out_ref[...] = x
            xn = _rms(x, g_ref[...]).astype(BF16)
            xn_ref[...] = xn
        else:
            xn = xn_ref[...]
        part = None
        for c in range(wg_ref.shape[1] // FFN_SUB):
            cols = slice(c * FFN_SUB, (c + 1) * FFN_SUB)
            gate = jnp.dot(xn, wg_ref[:, cols], preferred_element_type=F32)
            up = jnp.dot(xn, wu_ref[:, cols], preferred_element_type=F32)
            act = (gate * _sigmoid(gate) * up).astype(BF16)
            d = jnp.dot(act, wo_ref[cols, :], preferred_element_type=F32)
            part = d if part is None else part + d
        if first and not attn_in:
            out_ref[...] = h_ref[...] + 0.5 * part
        else:
            out_ref[...] += 0.5 * part

    pl.when(f == 0)(functools.partial(step, True))
    pl.when(f > 0)(functools.partial(step, False))

    @pl.when(f == n_f - 1)
    def _():
        hn = out_ref[...]
        if ple:
            xg = _rms(hn, pn_ref[...]).astype(BF16)
            gt = _sigmoid(jnp.dot(xg, pwg_ref[...], preferred_element_type=F32))
            pr = jnp.dot(p_ref[...].astype(BF16), pwp_ref[...], preferred_element_type=F32)
            hn = hn + gt * pr
        if final:
            hn = _rms(hn, fn_ref[...])
        out_ref[...] = hn
        if q_out:
            xq = _rms(hn, qg_ref[...]).astype(BF16)
            q_ref[...] = jnp.dot(xq, wq_ref[...], preferred_element_type=F32).astype(BF16)


def _ffn(h, layer, norm_g, w_in, w_out, attn_in=None, ple=None, final_g=None, q_out=None,
         cast_next=None):
    t = h.shape[0]
    tm = FFN_TM
    bare = attn_in is None and ple is None and q_out is None
    tf = FFN_TF_BARE if bare else FFN_TF
    n_t, n_f = t // tm, D_FF // tf
    row = lambda i, f: (i, 0)
    fixed = lambda i, f: (0, 0)
    once = pl.Buffered(1)
    vec = pl.BlockSpec((1, D_MODEL), fixed, pipeline_mode=once)

    def square(lyr):
        return pl.BlockSpec((None, D_MODEL, D_MODEL), lambda i, f: (lyr, 0, 0), pipeline_mode=once)

    in_specs = [
        pl.BlockSpec((tm, D_MODEL), row),
        vec,
        pl.BlockSpec((None, D_MODEL, tf), lambda i, f: (layer, 0, f)),
        pl.BlockSpec((None, D_MODEL, tf), lambda i, f: (layer, 0, f + n_f)),
        pl.BlockSpec((None, tf, D_MODEL), lambda i, f: (layer, f, 0)),
    ]
    args = [h, norm_g.reshape(1, D_MODEL), w_in, w_in, w_out]
    if attn_in is not None:
        o, proj_layer, w_proj = attn_in
        in_specs += [pl.BlockSpec((tm, V_DIM), row), square(proj_layer)]
        args += [o, w_proj]
    if ple is not None:
        pn, ple_layer, pwg, p, pwp = ple
        in_specs += [
            vec,
            square(ple_layer),
            pl.BlockSpec((None, tm, PLE_DIM), lambda i, f: (ple_layer, i, 0)),
            pl.BlockSpec((None, PLE_DIM, D_MODEL), lambda i, f: (ple_layer, 0, 0),
                         pipeline_mode=once),
        ]
        args += [pn.reshape(1, D_MODEL), pwg, p, pwp]
    if final_g is not None:
        in_specs.append(vec)
        args.append(final_g.reshape(1, D_MODEL))
    if q_out is not None:
        g_q, q_layer, w_q = q_out
        in_specs += [vec, square(q_layer)]
        args += [g_q.reshape(1, D_MODEL), w_q]
    out_specs = [pl.BlockSpec((tm, D_MODEL), row)]
    out_shape = [jax.ShapeDtypeStruct((t, D_MODEL), F32)]
    if q_out is not None:
        out_specs.append(pl.BlockSpec((tm, QK_DIM), row))
        out_shape.append(jax.ShapeDtypeStruct((t, QK_DIM), BF16))
    if cast_next is not None:
        nw_in, nw_out, n_layer = cast_next
        steps = n_t * n_f
        r_in, r_out = D_MODEL // steps, D_FF // steps
        slab = lambda i, f: (i * n_f + f, 0)
        in_specs += [
            pl.BlockSpec((None, r_in, 2 * D_FF), lambda i, f: (n_layer, i * n_f + f, 0)),
            pl.BlockSpec((None, r_out, D_MODEL), lambda i, f: (n_layer, i * n_f + f, 0)),
        ]
        args += [nw_in, nw_out]
        out_specs += [pl.BlockSpec((r_in, 2 * D_FF), slab), pl.BlockSpec((r_out, D_MODEL), slab)]
        out_shape += [jax.ShapeDtypeStruct((D_MODEL, 2 * D_FF), BF16),
                      jax.ShapeDtypeStruct((D_FF, D_MODEL), BF16)]
    body = functools.partial(_ffn_body, n_f=n_f, attn_in=attn_in is not None, ple=ple is not None,
                             final=final_g is not None, q_out=q_out is not None,
                             cast=cast_next is not None)
    return pl.pallas_call(
        body,
        grid=(n_t, n_f),
        in_specs=in_specs,
        out_specs=out_specs,
        out_shape=out_shape,
        scratch_shapes=[pltpu.VMEM((tm, D_MODEL), BF16)],
        compiler_params=_params(("parallel", "arbitrary")),
        name="ffn",
    )(*args)


def _conv_body(h_ref, g_ref, win_ref, bin_ref, wdw_ref, bdw_ref, lng_ref, lnb_ref,
               wout_ref, bout_ref, out_ref, ubuf_ref, cbuf_ref, sh_ref, *, ts):
    n_lb = D_MODEL // LANES
    halo = CONV_HALO
    first_tap_row = halo - (CONV_WIDTH - 1)

    @pl.when(pl.program_id(1) == 0)
    def _():
        ubuf_ref[:, 0:halo, :] = jnp.zeros((n_lb, halo, LANES), F32)

    x = h_ref[...]
    xn = _rms(x, g_ref[...]).astype(BF16)
    y = jnp.dot(xn, win_ref[...], preferred_element_type=F32) + bin_ref[...]
    u = y[:, :D_MODEL] * _sigmoid(y[:, D_MODEL:])
    for j in range(n_lb):
        ubuf_ref[j, halo:halo + ts, :] = u[:, j * LANES:(j + 1) * LANES]

    def lane_block(j, carry):
        for r in range(PACK):
            n = ts + (PACK if r < PACK - 1 else 0)
            sh_ref[r, 0:n, :] = ubuf_ref[j, pl.ds(first_tap_row + r, n), :].astype(BF16)
        for g in range(ts // PACK):
            acc = None
            for k in range(CONV_WIDTH):
                a, r = divmod(k, PACK)
                seg = sh_ref[r, (g + a) * PACK:(g + a + 1) * PACK, :]
                term = seg.astype(F32) * wdw_ref[j, k].astype(F32)
                acc = term if acc is None else acc + term
            cbuf_ref[j, g * PACK:(g + 1) * PACK, :] = acc
        return carry

    lax.fori_loop(0, n_lb, lane_block, 0)

    ubuf_ref[:, 0:halo, :] = ubuf_ref[:, ts:ts + halo, :]

    c = jnp.concatenate([cbuf_ref[j] for j in range(n_lb)], axis=1) + bdw_ref[...]
    mu = jnp.mean(c, axis=-1, keepdims=True)
    d = c - mu
    var = jnp.mean(d * d, axis=-1, keepdims=True)
    z = d * lax.rsqrt(var + LN_EPS) * lng_ref[...] + lnb_ref[...]
    z = (z * _sigmoid(z)).astype(BF16)
    out_ref[...] = x + jnp.dot(z, wout_ref[...], preferred_element_type=F32) + bout_ref[...]


def _conv_layer(h, batch, seq, layer, norm_g, w_in, b_in, w_dw, b_dw, ln_g, ln_b, w_out, b_out):
    ts = CONV_TS
    n_s = seq // ts
    n_lb = D_MODEL // LANES
    wdw = jnp.broadcast_to(
        w_dw.astype(BF16).reshape(CONV_WIDTH, n_lb, 1, LANES).transpose(1, 0, 2, 3),
        (n_lb, CONV_WIDTH, PACK, LANES))
    row = lambda b, s: (b * n_s + s, 0)
    fixed = lambda b, s: (0, 0)
    vec = pl.BlockSpec((1, D_MODEL), fixed)
    return pl.pallas_call(
        functools.partial(_conv_body, ts=ts),
        grid=(batch, n_s),
        in_specs=[
            pl.BlockSpec((ts, D_MODEL), row),
            vec,
            pl.BlockSpec((None, D_MODEL, 2 * D_MODEL), lambda b, s: (layer, 0, 0)),
            pl.BlockSpec((1, 2 * D_MODEL), fixed),
            pl.BlockSpec((n_lb, CONV_WIDTH, PACK, LANES), lambda b, s: (0, 0, 0, 0)),
            vec, vec, vec,
            pl.BlockSpec((None, D_MODEL, D_MODEL), lambda b, s: (layer, 0, 0)),
            vec,
        ],
        out_specs=pl.BlockSpec((ts, D_MODEL), row),
        out_shape=jax.ShapeDtypeStruct(h.shape, F32),
        scratch_shapes=[
            pltpu.VMEM((n_lb, CONV_HALO + ts, LANES), F32),
            pltpu.VMEM((n_lb, ts, LANES), F32),
            pltpu.VMEM((PACK, ts + PACK, LANES), BF16),
        ],
        compiler_params=_params(("parallel", "arbitrary")),
        name="conformer_conv",
    )(h, norm_g.reshape(1, -1), w_in, b_in.reshape(1, -1), wdw, b_dw.reshape(1, -1),
      ln_g.reshape(1, -1), ln_b.reshape(1, -1), w_out, b_out.reshape(1, -1))


def _bucket_tiles(t):
    assert t >= MAX_DISTANCE
    qp = np.arange(t)[:, None]
    kp = np.arange(t)[None, :]
    max_exact = NUM_BUCKETS // 2

    def bucket(n):
        nf = np.maximum(n, 1).astype(np.float32)
        large = max_exact + (np.log(nf / max_exact) / math.log(MAX_DISTANCE / max_exact)
                             * (NUM_BUCKETS - max_exact)).astype(np.int32)
        return np.where(n < max_exact, n, np.minimum(large, NUM_BUCKETS - 1))

    diag = np.where(kp <= qp, bucket(np.maximum(qp - kp, 0)), -1)
    left = bucket(qp + t - kp)
    return np.stack([diag, left]).astype(np.int32)


def _bias_body(rb_ref, bucket_ref, out_ref):
    h = pl.program_id(0)
    far = rb_ref[NUM_BUCKETS - 1, h]
    bk = bucket_ref[...]
    acc = jnp.full(bk.shape, NEG_INF, F32)
    for b in range(NUM_BUCKETS):
        acc = jnp.where(bk == b, rb_ref[b, h] - far, acc)
    out_ref[...] = acc


def _rel_bias_tiles(rel_bias, t):
    return pl.pallas_call(
        _bias_body,
        grid=(N_HEADS,),
        in_specs=[
            pl.BlockSpec(memory_space=pltpu.SMEM),
            pl.BlockSpec((2, t, t), lambda h: (0, 0, 0)),
        ],
        out_specs=pl.BlockSpec((None, 2, t, t), lambda h: (h, 0, 0, 0)),
        out_shape=jax.ShapeDtypeStruct((N_HEADS, 2, t, t), F32),
        compiler_params=_params(("parallel",)),
        name="rel_bias_tiles",
    )(rel_bias.astype(F32), jnp.asarray(_bucket_tiles(t)))


def _attn_body(q_ref, kt_ref, v_ref, bias_ref, lam_ref, sub_ref, out_ref, vaug_ref,
               *, t, n_q, lambda_init):
    seq = v_ref.shape[0]
    lq = lam_ref[...]
    lam = (jnp.exp(jnp.sum(lq[0:1] * lq[1:2], axis=-1, keepdims=True))
           - jnp.exp(jnp.sum(lq[2:3] * lq[3:4], axis=-1, keepdims=True)) + lambda_init)
    for hh in range(ATT_HEADS):
        vaug_ref[hh, :, :LANES] = v_ref[:, hh * LANES:(hh + 1) * LANES]
        vaug_ref[hh, :, LANES:] = jnp.ones((seq, LANES), BF16)

    for hh, i in [(hh, i) for hh in range(ATT_HEADS) for i in reversed(range(n_q))]:
        head = slice(hh * LANES, (hh + 1) * LANES)
        q = q_ref[i * t:(i + 1) * t, head]
        lane = lax.broadcasted_iota(jnp.int32, q.shape, 1)
        zero = jnp.zeros_like(q)
        q12 = jnp.concatenate([jnp.where(lane < HEAD_DIM, q, zero),
                               jnp.where(lane >= HEAD_DIM, q, zero)], axis=0)
        pieces = []
        if i >= 2:
            pieces.append((0, (i - 1) * t, None))
        if i >= 1:
            pieces.append(((i - 1) * t, i * t, 1))
        pieces.append((i * t, (i + 1) * t, 0))

        scores = []
        colmax = None
        for c0, c1, bias_idx in pieces:
            s = jnp.dot(q12, kt_ref[head, c0:c1], preferred_element_type=F32)
            if bias_idx is not None:
                s = (s.reshape(2, t, t) + bias_ref[hh, bias_idx][None]).reshape(2 * t, t)
            for c in range((c1 - c0) // LANES):
                blk = s[:, c * LANES:(c + 1) * LANES]
                colmax = blk if colmax is None else jnp.maximum(colmax, blk)
            scores.append(s)
        m = jnp.max(colmax, axis=1, keepdims=True)

        acc = None
        for (c0, c1, _), s in zip(pieces, scores):
            p = jnp.exp(s - m).astype(BF16)
            d = jnp.dot(p, vaug_ref[hh, c0:c1, :], preferred_element_type=F32)
            acc = d if acc is None else acc + d
        o12 = acc[:, :LANES] / acc[:, LANES:]
        o = o12[:t] - lam * o12[t:]
        out_ref[i * t:(i + 1) * t, head] = (
            _rms(o, sub_ref[...]) * (1.0 - lambda_init)).astype(out_ref.dtype)


def _attention(q, kt, v, bias_tiles, lam_rows, subln, batch, seq, lambda_init):
    t = ATT_T
    n_q = seq // t
    q3 = q.reshape(batch, seq, QK_DIM)
    width = ATT_HEADS * LANES
    head_cols = pl.BlockSpec((None, seq, width), lambda b, h: (b, 0, h))
    out = pl.pallas_call(
        functools.partial(_attn_body, t=t, n_q=n_q, lambda_init=lambda_init),
        grid=(batch, N_HEADS // ATT_HEADS),
        in_specs=[
            head_cols,
            pl.BlockSpec((None, width, seq), lambda b, h: (b, h, 0)),
            head_cols,
            pl.BlockSpec((ATT_HEADS, 2, t, t), lambda b, h: (h, 0, 0, 0)),
            pl.BlockSpec((4, HEAD_DIM), lambda b, h: (0, 0)),
            pl.BlockSpec((1, LANES), lambda b, h: (0, 0)),
        ],
        out_specs=head_cols,
        out_shape=jax.ShapeDtypeStruct((batch, seq, V_DIM), BF16),
        scratch_shapes=[pltpu.VMEM((ATT_HEADS, seq, 2 * LANES), BF16)],
        compiler_params=_params(("parallel", "parallel")),
        name="diff_attention",
    )(q3, kt, v, bias_tiles, lam_rows, subln.reshape(1, LANES))
    return out.reshape(batch * seq, V_DIM)


def _kv_body(h_ref, g_ref, wkt_ref, wv_ref, kt_ref, v_ref):
    xn = _rms(h_ref[...], g_ref[...]).astype(BF16)
    kt = lax.dot_general(wkt_ref[...], xn, (((1,), (1,)), ((), ())),
                         preferred_element_type=F32)
    kt_ref[...] = kt.astype(BF16)
    v_ref[...] = jnp.dot(xn, wv_ref[...], preferred_element_type=F32).astype(BF16)


def _shared_kv(h, batch, seq, norm_g, w_kt, w_v):
    tm = PROJ_TM
    n_s = seq // tm
    fixed = lambda b, s: (0, 0)
    return pl.pallas_call(
        _kv_body,
        grid=(batch, n_s),
        in_specs=[
            pl.BlockSpec((tm, D_MODEL), lambda b, s: (b * n_s + s, 0)),
            pl.BlockSpec((1, D_MODEL), fixed),
            pl.BlockSpec((QK_DIM, D_MODEL), fixed),
            pl.BlockSpec((D_MODEL, V_DIM), fixed),
        ],
        out_specs=[
            pl.BlockSpec((None, QK_DIM, tm), lambda b, s: (b, 0, s)),
            pl.BlockSpec((None, tm, V_DIM), lambda b, s: (b, s, 0)),
        ],
        out_shape=[
            jax.ShapeDtypeStruct((batch, QK_DIM, seq), BF16),
            jax.ShapeDtypeStruct((batch, seq, V_DIM), BF16),
        ],
        compiler_params=_params(("parallel", "parallel")),
        name="shared_kv",
    )(h, norm_g.reshape(1, D_MODEL), w_kt, w_v)


def _head_major(w):
    lead = w.shape[:-1]
    w = w.reshape(*lead, 2, N_HEADS, HEAD_DIM)
    return jnp.swapaxes(w, -3, -2).reshape(*lead, QK_DIM)


def kernel(x, p, ffn1_norm, ffn1_w_in, ffn1_w_out, mix_norm, ffn2_norm, ffn2_w_in, ffn2_w_out,
           ple_norm, ple_w_gate, ple_w_proj, conv_w_in, conv_b_in, conv_w_dw, conv_b_dw,
           conv_ln_g, conv_ln_b, conv_w_out, conv_b_out, kv_norm, w_kv, attn_w_q,
           attn_lq1, attn_lk1, attn_lq2, attn_lk2, attn_subln, attn_w_o, rel_bias, final_norm):
    batch, seq, _ = x.shape
    tokens = batch * seq
    h = x.reshape(tokens, D_MODEL)
    p2 = p.reshape(DEPTH, tokens, PLE_DIM)
    bias_tiles = _rel_bias_tiles(rel_bias, ATT_T)
    w_kt = _head_major(w_kv[:, :QK_DIM]).T.astype(BF16)
    w_v = w_kv[:, QK_DIM:].astype(BF16)
    ple_w_gate, ple_w_proj = ple_w_gate.astype(BF16), ple_w_proj.astype(BF16)
    conv_w_in, conv_w_out = conv_w_in.astype(BF16), conv_w_out.astype(BF16)
    attn_w_o = attn_w_o.astype(BF16)
    w_q = (_head_major(attn_w_q) * (HEAD_DIM ** -0.5)).astype(BF16)
    w_in, w_out = ffn1_w_in[0:1].astype(BF16), ffn1_w_out[0:1].astype(BF16)
    kt = v = o = None
    for i in range(DEPTH):
        j = i - N_A
        if i == N_A:
            kt, v = _shared_kv(h, batch, seq, kv_norm, w_kt, w_v)
        res = _ffn(h, 0, ffn1_norm[i], w_in, w_out,
                   q_out=(mix_norm[i], j, w_q) if i >= N_A else None,
                   cast_next=(ffn2_w_in, ffn2_w_out, i))
        h, w_in, w_out = res[0], res[-2][None], res[-1][None]
        if i < N_A:
            h = _conv_layer(h, batch, seq, i, mix_norm[i], conv_w_in, conv_b_in[i],
                            conv_w_dw[i], conv_b_dw[i], conv_ln_g[i], conv_ln_b[i],
                            conv_w_out, conv_b_out[i])
        else:
            lambda_init = 0.8 - 0.6 * math.exp(-0.3 * i)
            lam_rows = jnp.stack([attn_lq1[j], attn_lk1[j], attn_lq2[j], attn_lk2[j]]).astype(F32)
            o = _attention(res[1], kt, v, bias_tiles, lam_rows, attn_subln[j], batch, seq,
                           lambda_init)
        last = i == DEPTH - 1
        res = _ffn(h, 0, ffn2_norm[i], w_in, w_out,
                   attn_in=(o, j, attn_w_o) if i >= N_A else None,
                   ple=(ple_norm[i], i, ple_w_gate, p2, ple_w_proj),
                   final_g=final_norm if last else None,
                   cast_next=None if last else (ffn1_w_in, ffn1_w_out, i + 1))
        h = res[0]
        if not last:
            w_in, w_out = res[-2][None], res[-1][None]
    return h.reshape(batch, seq, D_MODEL)
```

```python
import functools
import math

import jax
import jax.numpy as jnp
import numpy as np
from jax import lax
from jax.experimental import pallas as pl
from jax.experimental.pallas import tpu as pltpu

D_MODEL = 1024
DEPTH = 4
N_A = DEPTH // 2
D_FF = 4 * D_MODEL
CONV_WIDTH = 31
HEAD_DIM = 64
N_HEADS = D_MODEL // (2 * HEAD_DIM)
QK_DIM = 2 * N_HEADS * HEAD_DIM
V_DIM = N_HEADS * 2 * HEAD_DIM
NUM_BUCKETS = 32
MAX_DISTANCE = 128
PLE_DIM = 256
RMS_EPS = 1e-6
LN_EPS = 1e-5
NEG_INF = -1e30

LANES = 128
PACK = 16
VMEM_LIMIT_BYTES = 56 * 1024 * 1024

FFN_TM = 1024
FFN_TF = 1024
FFN_TF_BARE = 2048
FFN_SUB = 256
PROJ_TM = 1024
CONV_TS = 1024
CONV_HALO = 32
ATT_T = 256
ATT_HEADS = 2

F32 = jnp.float32
BF16 = jnp.bfloat16


def _rms(x, g):
    ms = jnp.mean(x * x, axis=-1, keepdims=True)
    return x * lax.rsqrt(ms + RMS_EPS) * g


def _sigmoid(x):
    return 1.0 / (1.0 + jnp.exp(-x))


def _params(sem):
    return pltpu.CompilerParams(dimension_semantics=sem, vmem_limit_bytes=VMEM_LIMIT_BYTES)


def _ffn_body(*refs, n_f, attn_in, ple, final, q_out, cast):
    it = iter(refs)
    h_ref, g_ref, wg_ref, wu_ref, wo_ref = (next(it) for _ in range(5))
    if attn_in:
        o_ref, wproj_ref = next(it), next(it)
    if ple:
        pn_ref, pwg_ref, p_ref, pwp_ref = (next(it) for _ in range(4))
    if final:
        fn_ref = next(it)
    if q_out:
        qg_ref, wq_ref = next(it), next(it)
    if cast:
        nwin_ref, nwout_ref = next(it), next(it)
    out_ref = next(it)
    if q_out:
        q_ref = next(it)
    if cast:
        cwin_ref, cwout_ref = next(it), next(it)
    xn_ref = next(it)

    f = pl.program_id(1)

    if cast:
        cwin_ref[...] = nwin_ref[...].astype(BF16)
        cwout_ref[...] = nwout_ref[...].astype(BF16)

    def step(first):
        if first:
            x = h_ref[...]
            if attn_in:
                x = x + jnp.dot(o_ref[...], wproj_ref[...], preferred_element_type=F32)
                out_ref[...] = x
            xn = _rms(x, g_ref[...]).astype(BF16)
            xn_ref[...] = xn
        else:
            xn = xn_ref[...]
        part = None
        for c in range(wg_ref.shape[1] // FFN_SUB):
            cols = slice(c * FFN_SUB, (c + 1) * FFN_SUB)
            gate = jnp.dot(xn, wg_ref[:, cols], preferred_element_type=F32)
            up = jnp.dot(xn, wu_ref[:, cols], preferred_element_type=F32)
            act = (gate * _sigmoid(gate) * up).astype(BF16)
            d = jnp.dot(act, wo_ref[cols, :], preferred_element_type=F32)
            part = d if part is None else part + d
        if first and not attn_in:
            out_ref[...] = h_ref[...] + 0.5 * part
        else:
            out_ref[...] += 0.5 * part

    pl.when(f == 0)(functools.partial(step, True))
    pl.when(f > 0)(functools.partial(step, False))

    @pl.when(f == n_f - 1)
    def _():
        hn = out_ref[...]
        if ple:
            xg = _rms(hn, pn_ref[...]).astype(BF16)
            gt = _sigmoid(jnp.dot(xg, pwg_ref[...], preferred_element_type=F32))
            pr = jnp.dot(p_ref[...].astype(BF16), pwp_ref[...], preferred_element_type=F32)
            hn = hn + gt * pr
        if final:
            hn = _rms(hn, fn_ref[...])
        out_ref[...] = hn
        if q_out:
            xq = _rms(hn, qg_ref[...]).astype(BF16)
            q_ref[...] = jnp.dot(xq, wq_ref[...], preferred_element_type=F32).astype(BF16)


def _ffn(h, layer, norm_g, w_in, w_out, attn_in=None, ple=None, final_g=None, q_out=None,
         cast_next=None):
    t = h.shape[0]
    tm = FFN_TM
    bare = attn_in is None and ple is None and q_out is None
    tf = FFN_TF_BARE if bare else FFN_TF
    n_t, n_f = t // tm, D_FF // tf
    row = lambda i, f: (i, 0)
    fixed = lambda i, f: (0, 0)
    once = pl.Buffered(1)
    vec = pl.BlockSpec((1, D_MODEL), fixed, pipeline_mode=once)

    def square(lyr):
        return pl.BlockSpec((None, D_MODEL, D_MODEL), lambda i, f: (lyr, 0, 0), pipeline_mode=once)

    in_specs = [
        pl.BlockSpec((tm, D_MODEL), row),
        vec,
        pl.BlockSpec((None, D_MODEL, tf), lambda i, f: (layer, 0, f)),
        pl.BlockSpec((None, D_MODEL, tf), lambda i, f: (layer, 0, f + n_f)),
        pl.BlockSpec((None, tf, D_MODEL), lambda i, f: (layer, f, 0)),
    ]
    args = [h, norm_g.reshape(1, D_MODEL), w_in, w_in, w_out]
    if attn_in is not None:
        o, proj_layer, w_proj = attn_in
        in_specs += [pl.BlockSpec((tm, V_DIM), row), square(proj_layer)]
        args += [o, w_proj]
    if ple is not None:
        pn, ple_layer, pwg, p, pwp = ple
        in_specs += [
            vec,
            square(ple_layer),
            pl.BlockSpec((None, tm, PLE_DIM), lambda i, f: (ple_layer, i, 0)),
            pl.BlockSpec((None, PLE_DIM, D_MODEL), lambda i, f: (ple_layer, 0, 0),
                         pipeline_mode=once),
        ]
        args += [pn.reshape(1, D_MODEL), pwg, p, pwp]
    if final_g is not None:
        in_specs.append(vec)
        args.append(final_g.reshape(1, D_MODEL))
    if q_out is not None:
        g_q, q_layer, w_q = q_out
        in_specs += [vec, square(q_layer)]
        args += [g_q.reshape(1, D_MODEL), w_q]
    out_specs = [pl.BlockSpec((tm, D_MODEL), row)]
    out_shape = [jax.ShapeDtypeStruct((t, D_MODEL), F32)]
    if q_out is not None:
        out_specs.append(pl.BlockSpec((tm, QK_DIM), row))
        out_shape.append(jax.ShapeDtypeStruct((t, QK_DIM), BF16))
    if cast_next is not None:
        nw_in, nw_out, n_layer = cast_next
        steps = n_t * n_f
        r_in, r_out = D_MODEL // steps, D_FF // steps
        slab = lambda i, f: (i * n_f + f, 0)
        in_specs += [
            pl.BlockSpec((None, r_in, 2 * D_FF), lambda i, f: (n_layer, i * n_f + f, 0)),
            pl.BlockSpec((None, r_out, D_MODEL), lambda i, f: (n_layer, i * n_f + f, 0)),
        ]
        args += [nw_in, nw_out]
        out_specs += [pl.BlockSpec((r_in, 2 * D_FF), slab), pl.BlockSpec((r_out, D_MODEL), slab)]
        out_shape += [jax.ShapeDtypeStruct((D_MODEL, 2 * D_FF), BF16),
                      jax.ShapeDtypeStruct((D_FF, D_MODEL), BF16)]
    body = functools.partial(_ffn_body, n_f=n_f, attn_in=attn_in is not None, ple=ple is not None,
                             final=final_g is not None, q_out=q_out is not None,
                             cast=cast_next is not None)
    return pl.pallas_call(
        body,
        grid=(n_t, n_f),
        in_specs=in_specs,
        out_specs=out_specs,
        out_shape=out_shape,
        scratch_shapes=[pltpu.VMEM((tm, D_MODEL), BF16)],
        compiler_params=_params(("parallel", "arbitrary")),
        name="ffn",
    )(*args)


def _conv_body(h_ref, g_ref, win_ref, bin_ref, wdw_ref, bdw_ref, lng_ref, lnb_ref,
               wout_ref, bout_ref, out_ref, ubuf_ref, cbuf_ref, sh_ref, *, ts):
    n_lb = D_MODEL // LANES
    halo = CONV_HALO
    first_tap_row = halo - (CONV_WIDTH - 1)

    @pl.when(pl.program_id(1) == 0)
    def _():
        ubuf_ref[:, 0:halo, :] = jnp.zeros((n_lb, halo, LANES), F32)

    x = h_ref[...]
    xn = _rms(x, g_ref[...]).astype(BF16)
    y = jnp.dot(xn, win_ref[...], preferred_element_type=F32) + bin_ref[...]
    u = y[:, :D_MODEL] * _sigmoid(y[:, D_MODEL:])
    for j in range(n_lb):
        ubuf_ref[j, halo:halo + ts, :] = u[:, j * LANES:(j + 1) * LANES]

    def lane_block(j, carry):
        for r in range(PACK):
            n = ts + (PACK if r < PACK - 1 else 0)
            sh_ref[r, 0:n, :] = ubuf_ref[j, pl.ds(first_tap_row + r, n), :].astype(BF16)
        for g in range(ts // PACK):
            acc = None
            for k in range(CONV_WIDTH):
                a, r = divmod(k, PACK)
                seg = sh_ref[r, (g + a) * PACK:(g + a + 1) * PACK, :]
                term = seg.astype(F32) * wdw_ref[j, k].astype(F32)
                acc = term if acc is None else acc + term
            cbuf_ref[j, g * PACK:(g + 1) * PACK, :] = acc
        return carry

    lax.fori_loop(0, n_lb, lane_block, 0)

    ubuf_ref[:, 0:halo, :] = ubuf_ref[:, ts:ts + halo, :]

    c = jnp.concatenate([cbuf_ref[j] for j in range(n_lb)], axis=1) + bdw_ref[...]
    mu = jnp.mean(c, axis=-1, keepdims=True)
    d = c - mu
    var = jnp.mean(d * d, axis=-1, keepdims=True)
    z = d * lax.rsqrt(var + LN_EPS) * lng_ref[...] + lnb_ref[...]
    z = (z * _sigmoid(z)).astype(BF16)
    out_ref[...] = x + jnp.dot(z, wout_ref[...], preferred_element_type=F32) + bout_ref[...]


def _conv_layer(h, batch, seq, layer, norm_g, w_in, b_in, w_dw, b_dw, ln_g, ln_b, w_out, b_out):
    ts = CONV_TS
    n_s = seq // ts
    n_lb = D_MODEL // LANES
    wdw = jnp.broadcast_to(
        w_dw.astype(BF16).reshape(CONV_WIDTH, n_lb, 1, LANES).transpose(1, 0, 2, 3),
        (n_lb, CONV_WIDTH, PACK, LANES))
    row = lambda b, s: (b * n_s + s, 0)
    fixed = lambda b, s: (0, 0)
    vec = pl.BlockSpec((1, D_MODEL), fixed)
    return pl.pallas_call(
        functools.partial(_conv_body, ts=ts),
        grid=(batch, n_s),
        in_specs=[
            pl.BlockSpec((ts, D_MODEL), row),
            vec,
            pl.BlockSpec((None, D_MODEL, 2 * D_MODEL), lambda b, s: (layer, 0, 0)),
            pl.BlockSpec((1, 2 * D_MODEL), fixed),
            pl.BlockSpec((n_lb, CONV_WIDTH, PACK, LANES), lambda b, s: (0, 0, 0, 0)),
            vec, vec, vec,
            pl.BlockSpec((None, D_MODEL, D_MODEL), lambda b, s: (layer, 0, 0)),
            vec,
        ],
        out_specs=pl.BlockSpec((ts, D_MODEL), row),
        out_shape=jax.ShapeDtypeStruct(h.shape, F32),
        scratch_shapes=[
            pltpu.VMEM((n_lb, CONV_HALO + ts, LANES), F32),
            pltpu.VMEM((n_lb, ts, LANES), F32),
            pltpu.VMEM((PACK, ts + PACK, LANES), BF16),
        ],
        compiler_params=_params(("parallel", "arbitrary")),
        name="conformer_conv",
    )(h, norm_g.reshape(1, -1), w_in, b_in.reshape(1, -1), wdw, b_dw.reshape(1, -1),
      ln_g.reshape(1, -1), ln_b.reshape(1, -1), w_out, b_out.reshape(1, -1))


def _bucket_tiles(t):
    assert t >= MAX_DISTANCE
    qp = np.arange(t)[:, None]
    kp = np.arange(t)[None, :]
    max_exact = NUM_BUCKETS // 2

    def bucket(n):
        nf = np.maximum(n, 1).astype(np.float32)
        large = max_exact + (np.log(nf / max_exact) / math.log(MAX_DISTANCE / max_exact)
                             * (NUM_BUCKETS - max_exact)).astype(np.int32)
        return np.where(n < max_exact, n, np.minimum(large, NUM_BUCKETS - 1))

    diag = np.where(kp <= qp, bucket(np.maximum(qp - kp, 0)), -1)
    left = bucket(qp + t - kp)
    return np.stack([diag, left]).astype(np.int32)


def _bias_body(rb_ref, bucket_ref, out_ref):
    h = pl.program_id(0)
    far = rb_ref[NUM_BUCKETS - 1, h]
    bk = bucket_ref[...]
    acc = jnp.full(bk.shape, NEG_INF, F32)
    for b in range(NUM_BUCKETS):
        acc = jnp.where(bk == b, rb_ref[b, h] - far, acc)
    out_ref[...] = acc


def _rel_bias_tiles(rel_bias, t):
    return pl.pallas_call(
        _bias_body,
        grid=(N_HEADS,),
        in_specs=[
            pl.BlockSpec(memory_space=pltpu.SMEM),
            pl.BlockSpec((2, t, t), lambda h: (0, 0, 0)),
        ],
        out_specs=pl.BlockSpec((None, 2, t, t), lambda h: (h, 0, 0, 0)),
        out_shape=jax.ShapeDtypeStruct((N_HEADS, 2, t, t), F32),
        compiler_params=_params(("parallel",)),
        name="rel_bias_tiles",
    )(rel_bias.astype(F32), jnp.asarray(_bucket_tiles(t)))


def _attn_body(q_ref, kt_ref, v_ref, bias_ref, lam_ref, sub_ref, out_ref, vaug_ref,
               *, t, n_q, lambda_init):
    seq = v_ref.shape[0]
    lq = lam_ref[...]
    lam = (jnp.exp(jnp.sum(lq[0:1] * lq[1:2], axis=-1, keepdims=True))
           - jnp.exp(jnp.sum(lq[2:3] * lq[3:4], axis=-1, keepdims=True)) + lambda_init)
    for hh in range(ATT_HEADS):
        vaug_ref[hh, :, :LANES] = v_ref[:, hh * LANES:(hh + 1) * LANES]
        vaug_ref[hh, :, LANES:] = jnp.ones((seq, LANES), BF16)

    for hh, i in [(hh, i) for hh in range(ATT_HEADS) for i in reversed(range(n_q))]:
        head = slice(hh * LANES, (hh + 1) * LANES)
        q = q_ref[i * t:(i + 1) * t, head]
        lane = lax.broadcasted_iota(jnp.int32, q.shape, 1)
        zero = jnp.zeros_like(q)
        q12 = jnp.concatenate([jnp.where(lane < HEAD_DIM, q, zero),
                               jnp.where(lane >= HEAD_DIM, q, zero)], axis=0)
        pieces = []
        if i >= 2:
            pieces.append((0, (i - 1) * t, None))
        if i >= 1:
            pieces.append(((i - 1) * t, i * t, 1))
        pieces.append((i * t, (i + 1) * t, 0))

        scores = []
        colmax = None
        for c0, c1, bias_idx in pieces:
            s = jnp.dot(q12, kt_ref[head, c0:c1], preferred_element_type=F32)
            if bias_idx is not None:
                s = (s.reshape(2, t, t) + bias_ref[hh, bias_idx][None]).reshape(2 * t, t)
            for c in range((c1 - c0) // LANES):
                blk = s[:, c * LANES:(c + 1) * LANES]
                colmax = blk if colmax is None else jnp.maximum(colmax, blk)
            scores.append(s)
        m = jnp.max(colmax, axis=1, keepdims=True)

        acc = None
        for (c0, c1, _), s in zip(pieces, scores):
            p = jnp.exp(s - m).astype(BF16)
            d = jnp.dot(p, vaug_ref[hh, c0:c1, :], preferred_element_type=F32)
            acc = d if acc is None else acc + d
        o12 = acc[:, :LANES] / acc[:, LANES:]
        o = o12[:t] - lam * o12[t:]
        out_ref[i * t:(i + 1) * t, head] = (
            _rms(o, sub_ref[...]) * (1.0 - lambda_init)).astype(out_ref.dtype)


def _attention(q, kt, v, bias_tiles, lam_rows, subln, batch, seq, lambda_init):
    t = ATT_T
    n_q = seq // t
    q3 = q.reshape(batch, seq, QK_DIM)
    width = ATT_HEADS * LANES
    head_cols = pl.BlockSpec((None, seq, width), lambda b, h: (b, 0, h))
    out = pl.pallas_call(
        functools.partial(_attn_body, t=t, n_q=n_q, lambda_init=lambda_init),
        grid=(batch, N_HEADS // ATT_HEADS),
        in_specs=[
            head_cols,
            pl.BlockSpec((None, width, seq), lambda b, h: (b, h, 0)),
            head_cols,
            pl.BlockSpec((ATT_HEADS, 2, t, t), lambda b, h: (h, 0, 0, 0)),
            pl.BlockSpec((4, HEAD_DIM), lambda b, h: (0, 0)),
            pl.BlockSpec((1, LANES), lambda b, h: (0, 0)),
        ],
        out_specs=head_cols,
        out_shape=jax.ShapeDtypeStruct((batch, seq, V_DIM), BF16),
        scratch_shapes=[pltpu.VMEM((ATT_HEADS, seq, 2 * LANES), BF16)],
        compiler_params=_params(("parallel", "parallel")),
        name="diff_attention",
    )(q3, kt, v, bias_tiles, lam_rows, subln.reshape(1, LANES))
    return out.reshape(batch * seq, V_DIM)


def _kv_body(h_ref, g_ref, wkt_ref, wv_ref, kt_ref, v_ref):
    xn = _rms(h_ref[...], g_ref[...]).astype(BF16)
    kt = lax.dot_general(wkt_ref[...], xn, (((1,), (1,)), ((), ())),
                         preferred_element_type=F32)
    kt_ref[...] = kt.astype(BF16)
    v_ref[...] = jnp.dot(xn, wv_ref[...], preferred_element_type=F32).astype(BF16)


def _shared_kv(h, batch, seq, norm_g, w_kt, w_v):
    tm = PROJ_TM
    n_s = seq // tm
    fixed = lambda b, s: (0, 0)
    return pl.pallas_call(
        _kv_body,
        grid=(batch, n_s),
        in_specs=[
            pl.BlockSpec((tm, D_MODEL), lambda b, s: (b * n_s + s, 0)),
            pl.BlockSpec((1, D_MODEL), fixed),
            pl.BlockSpec((QK_DIM, D_MODEL), fixed),
            pl.BlockSpec((D_MODEL, V_DIM), fixed),
        ],
        out_specs=[
            pl.BlockSpec((None, QK_DIM, tm), lambda b, s: (b, 0, s)),
            pl.BlockSpec((None, tm, V_DIM), lambda b, s: (b, s, 0)),
        ],
        out_shape=[
            jax.ShapeDtypeStruct((batch, QK_DIM, seq), BF16),
            jax.ShapeDtypeStruct((batch, seq, V_DIM), BF16),
        ],
        compiler_params=_params(("parallel", "parallel")),
        name="shared_kv",
    )(h, norm_g.reshape(1, D_MODEL), w_kt, w_v)


def _head_major(w):
    lead = w.shape[:-1]
    w = w.reshape(*lead, 2, N_HEADS, HEAD_DIM)
    return jnp.swapaxes(w, -3, -2).reshape(*lead, QK_DIM)


def kernel(x, p, ffn1_norm, ffn1_w_in, ffn1_w_out, mix_norm, ffn2_norm, ffn2_w_in, ffn2_w_out,
           ple_norm, ple_w_gate, ple_w_proj, conv_w_in, conv_b_in, conv_w_dw, conv_b_dw,
           conv_ln_g, conv_ln_b, conv_w_out, conv_b_out, kv_norm, w_kv, attn_w_q,
           attn_lq1, attn_lk1, attn_lq2, attn_lk2, attn_subln, attn_w_o, rel_bias, final_norm):
    batch, seq, _ = x.shape
    tokens = batch * seq
    h = x.reshape(tokens, D_MODEL)
    p2 = p.reshape(DEPTH, tokens, PLE_DIM)
    bias_tiles = _rel_bias_tiles(rel_bias, ATT_T)
    w_kt = _head_major(w_kv[:, :QK_DIM]).T.astype(BF16)
    w_v = w_kv[:, QK_DIM:].astype(BF16)
    ple_w_gate, ple_w_proj = ple_w_gate.astype(BF16), ple_w_proj.astype(BF16)
    conv_w_in, conv_w_out = conv_w_in.astype(BF16), conv_w_out.astype(BF16)
    attn_w_o = attn_w_o.astype(BF16)
    w_q = (_head_major(attn_w_q) * (HEAD_DIM ** -0.5)).astype(BF16)
    w_in, w_out = ffn1_w_in[0:1].astype(BF16), ffn1_w_out[0:1].astype(BF16)
    kt = v = o = None
    for i in range(DEPTH):
        j = i - N_A
        if i == N_A:
            kt, v = _shared_kv(h, batch, seq, kv_norm, w_kt, w_v)
        res = _ffn(h, 0, ffn1_norm[i], w_in, w_out,
                   q_out=(mix_norm[i], j, w_q) if i >= N_A else None,
                   cast_next=(ffn2_w_in, ffn2_w_out, i))
        h, w_in, w_out = res[0], res[-2][None], res[-1][None]
        if i < N_A:
            h = _conv_layer(h, batch, seq, i, mix_norm[i], conv_w_in, conv_b_in[i],
                            conv_w_dw[i], conv_b_dw[i], conv_ln_g[i], conv_ln_b[i],
                            conv_w_out, conv_b_out[i])
        else:
            lambda_init = 0.8 - 0.6 * math.exp(-0.3 * i)
            lam_rows = jnp.stack([attn_lq1[j], attn_lk1[j], attn_lq2[j], attn_lk2[j]]).astype(F32)
            o = _attention(res[1], kt, v, bias_tiles, lam_rows, attn_subln[j], batch, seq,
                           lambda_init)
        last = i == DEPTH - 1
        res = _ffn(h, 0, ffn2_norm[i], w_in, w_out,
                   attn_in=(o, j, attn_w_o) if i >= N_A else None,
                   ple=(ple_norm[i], i, ple_w_gate, p2, ple_w_proj),
                   final_g=final_norm if last else None,
                   cast_next=None if last else (ffn1_w_in, ffn1_w_out, i + 1))
        h = res[0]
        if not last:
            w_in, w_out = res[-2][None], res[-1][None]
    return h.reshape(batch, seq, D_MODEL)
```

```python
import functools
import math

import jax
import jax.numpy as jnp
import numpy as np
from jax import lax
from jax.experimental import pallas as pl
from jax.experimental.pallas import tpu as pltpu

D_MODEL = 1024
DEPTH = 4
N_A = DEPTH // 2
D_FF = 4 * D_MODEL
CONV_WIDTH = 31
HEAD_DIM = 64
N_HEADS = D_MODEL // (2 * HEAD_DIM)
QK_DIM = 2 * N_HEADS * HEAD_DIM
V_DIM = N_HEADS * 2 * HEAD_DIM
NUM_BUCKETS = 32
MAX_DISTANCE = 128
PLE_DIM = 256
RMS_EPS = 1e-6
LN_EPS = 1e-5
NEG_INF = -1e30

LANES = 128
PACK = 16
VMEM_LIMIT_BYTES = 56 * 1024 * 1024

FFN_TM = 1024
FFN_TF = 1024
FFN_TF_BARE = 2048
FFN_SUB = 256
FFN_CAST_SLABS = 32
PROJ_TM = 1024
CONV_TS = 1024
CONV_HALO = 32
ATT_T = 256
ATT_HEADS = 2

F32 = jnp.float32
BF16 = jnp.bfloat16


def _rms(x, g):
    ms = jnp.mean(x * x, axis=-1, keepdims=True)
    return x * lax.rsqrt(ms + RMS_EPS) * g


def _sigmoid(x):
    return 1.0 / (1.0 + jnp.exp(-x))


def _params(sem):
    return pltpu.CompilerParams(dimension_semantics=sem, vmem_limit_bytes=VMEM_LIMIT_BYTES)


def _ffn_body(*refs, n_f, attn_in, ple, final, q_out, cast):
    it = iter(refs)
    h_ref, g_ref, wg_ref, wu_ref, wo_ref = (next(it) for _ in range(5))
    if attn_in:
        o_ref, wproj_ref = next(it), next(it)
    if ple:
        pn_ref, pwg_ref, p_ref, pwp_ref = (next(it) for _ in range(4))
    if final:
        fn_ref = next(it)
    if q_out:
        qg_ref, wq_ref = next(it), next(it)
    if cast:
        nwin_ref, nwout_ref = next(it), next(it)
    out_ref = next(it)
    if q_out:
        q_ref = next(it)
    if cast:
        cwin_ref, cwout_ref = next(it), next(it)
    xn_ref = next(it)

    f = pl.program_id(1)

    if cast:
        cwin_ref[...] = nwin_ref[...].astype(BF16)
        cwout_ref[...] = nwout_ref[...].astype(BF16)

    def step(first):
        if first:
            x = h_ref[...]
            if attn_in:
                x = x + jnp.dot(o_ref[...], wproj_ref[...], preferred_element_type=F32)
                out_ref[...] = x
            xn = _rms(x, g_ref[...]).astype(BF16)
            xn_ref[...] = xn
        else:
            xn = xn_ref[...]
        part = None
        for c in range(wg_ref.shape[1] // FFN_SUB):
            cols = slice(c * FFN_SUB, (c + 1) * FFN_SUB)
            gate = jnp.dot(xn, wg_ref[:, cols], preferred_element_type=F32)
            up = jnp.dot(xn, wu_ref[:, cols], preferred_element_type=F32)
            act = (gate * _sigmoid(gate) * up).astype(BF16)
            d = jnp.dot(act, wo_ref[cols, :], preferred_element_type=F32)
            part = d if part is None else part + d
        if first and not attn_in:
            out_ref[...] = h_ref[...] + 0.5 * part
        else:
            out_ref[...] += 0.5 * part

    pl.when(f == 0)(functools.partial(step, True))
    pl.when(f > 0)(functools.partial(step, False))

    @pl.when(f == n_f - 1)
    def _():
        hn = out_ref[...]
        if ple:
            xg = _rms(hn, pn_ref[...]).astype(BF16)
            gt = _sigmoid(jnp.dot(xg, pwg_ref[...], preferred_element_type=F32))
            pr = jnp.dot(p_ref[...].astype(BF16), pwp_ref[...], preferred_element_type=F32)
            hn = hn + gt * pr
        if final:
            hn = _rms(hn, fn_ref[...])
        out_ref[...] = hn
        if q_out:
            xq = _rms(hn, qg_ref[...]).astype(BF16)
            q_ref[...] = jnp.dot(xq, wq_ref[...], preferred_element_type=F32).astype(BF16)


def _ffn(h, layer, norm_g, w_in, w_out, attn_in=None, ple=None, final_g=None, q_out=None,
         cast_next=None):
    t = h.shape[0]
    tm = FFN_TM
    bare = attn_in is None and ple is None and q_out is None
    tf = FFN_TF_BARE if bare else FFN_TF
    n_t, n_f = t // tm, D_FF // tf
    row = lambda i, f: (i, 0)
    fixed = lambda i, f: (0, 0)
    once = pl.Buffered(1)
    vec = pl.BlockSpec((1, D_MODEL), fixed, pipeline_mode=once)

    def square(lyr):
        return pl.BlockSpec((None, D_MODEL, D_MODEL), lambda i, f: (lyr, 0, 0), pipeline_mode=once)

    in_specs = [
        pl.BlockSpec((tm, D_MODEL), row),
        vec,
        pl.BlockSpec((None, D_MODEL, tf), lambda i, f: (layer, 0, f)),
        pl.BlockSpec((None, D_MODEL, tf), lambda i, f: (layer, 0, f + n_f)),
        pl.BlockSpec((None, tf, D_MODEL), lambda i, f: (layer, f, 0)),
    ]
    args = [h, norm_g.reshape(1, D_MODEL), w_in, w_in, w_out]
    if attn_in is not None:
        o, proj_layer, w_proj = attn_in
        in_specs += [pl.BlockSpec((tm, V_DIM), row), square(proj_layer)]
        args += [o, w_proj]
    if ple is not None:
        pn, ple_layer, pwg, p, pwp = ple
        in_specs += [
            vec,
            square(ple_layer),
            pl.BlockSpec((None, tm, PLE_DIM), lambda i, f: (ple_layer, i, 0)),
            pl.BlockSpec((None, PLE_DIM, D_MODEL), lambda i, f: (ple_layer, 0, 0),
                         pipeline_mode=once),
        ]
        args += [pn.reshape(1, D_MODEL), pwg, p, pwp]
    if final_g is not None:
        in_specs.append(vec)
        args.append(final_g.reshape(1, D_MODEL))
    if q_out is not None:
        g_q, q_layer, w_q = q_out
        in_specs += [vec, square(q_layer)]
        args += [g_q.reshape(1, D_MODEL), w_q]
    out_specs = [pl.BlockSpec((tm, D_MODEL), row)]
    out_shape = [jax.ShapeDtypeStruct((t, D_MODEL), F32)]
    if q_out is not None:
        out_specs.append(pl.BlockSpec((tm, QK_DIM), row))
        out_shape.append(jax.ShapeDtypeStruct((t, QK_DIM), BF16))
    if cast_next is not None:
        nw_in, nw_out, n_layer = cast_next
        per_slab = n_t * n_f // FFN_CAST_SLABS
        r_in, r_out = D_MODEL // FFN_CAST_SLABS, D_FF // FFN_CAST_SLABS
        slab = lambda i, f: ((i * n_f + f) // per_slab, 0)
        in_specs += [
            pl.BlockSpec((None, r_in, 2 * D_FF), lambda i, f: (n_layer, *slab(i, f))),
            pl.BlockSpec((None, r_out, D_MODEL), lambda i, f: (n_layer, *slab(i, f))),
        ]
        args += [nw_in, nw_out]
        out_specs += [pl.BlockSpec((r_in, 2 * D_FF), slab), pl.BlockSpec((r_out, D_MODEL), slab)]
        out_shape += [jax.ShapeDtypeStruct((D_MODEL, 2 * D_FF), BF16),
                      jax.ShapeDtypeStruct((D_FF, D_MODEL), BF16)]
    body = functools.partial(_ffn_body, n_f=n_f, attn_in=attn_in is not None, ple=ple is not None,
                             final=final_g is not None, q_out=q_out is not None,
                             cast=cast_next is not None)
    return pl.pallas_call(
        body,
        grid=(n_t, n_f),
        in_specs=in_specs,
        out_specs=out_specs,
        out_shape=out_shape,
        scratch_shapes=[pltpu.VMEM((tm, D_MODEL), BF16)],
        compiler_params=_params(("parallel", "arbitrary")),
        name="ffn",
    )(*args)


def _conv_body(h_ref, g_ref, win_ref, bin_ref, wdw_ref, bdw_ref, lng_ref, lnb_ref,
               wout_ref, bout_ref, out_ref, ubuf_ref, cbuf_ref, sh_ref, *, ts):
    n_lb = D_MODEL // LANES
    halo = CONV_HALO
    first_tap_row = halo - (CONV_WIDTH - 1)

    @pl.when(pl.program_id(1) == 0)
    def _():
        ubuf_ref[:, 0:halo, :] = jnp.zeros((n_lb, halo, LANES), F32)

    x = h_ref[...]
    xn = _rms(x, g_ref[...]).astype(BF16)
    y = jnp.dot(xn, win_ref[...], preferred_element_type=F32) + bin_ref[...]
    u = y[:, :D_MODEL] * _sigmoid(y[:, D_MODEL:])
    for j in range(n_lb):
        ubuf_ref[j, halo:halo + ts, :] = u[:, j * LANES:(j + 1) * LANES]

    def lane_block(j, carry):
        for r in range(PACK):
            n = ts + (PACK if r < PACK - 1 else 0)
            sh_ref[r, 0:n, :] = ubuf_ref[j, pl.ds(first_tap_row + r, n), :].astype(BF16)
        for g in range(ts // PACK):
            acc = None
            for k in range(CONV_WIDTH):
                a, r = divmod(k, PACK)
                seg = sh_ref[r, (g + a) * PACK:(g + a + 1) * PACK, :]
                term = seg.astype(F32) * wdw_ref[j, k].astype(F32)
                acc = term if acc is None else acc + term
            cbuf_ref[j, g * PACK:(g + 1) * PACK, :] = acc
        return carry

    lax.fori_loop(0, n_lb, lane_block, 0)

    ubuf_ref[:, 0:halo, :] = ubuf_ref[:, ts:ts + halo, :]

    c = jnp.concatenate([cbuf_ref[j] for j in range(n_lb)], axis=1) + bdw_ref[...]
    mu = jnp.mean(c, axis=-1, keepdims=True)
    d = c - mu
    var = jnp.mean(d * d, axis=-1, keepdims=True)
    z = d * lax.rsqrt(var + LN_EPS) * lng_ref[...] + lnb_ref[...]
    z = (z * _sigmoid(z)).astype(BF16)
    out_ref[...] = x + jnp.dot(z, wout_ref[...], preferred_element_type=F32) + bout_ref[...]


def _conv_layer(h, batch, seq, layer, norm_g, w_in, b_in, w_dw, b_dw, ln_g, ln_b, w_out, b_out):
    ts = CONV_TS
    n_s = seq // ts
    n_lb = D_MODEL // LANES
    wdw = jnp.broadcast_to(
        w_dw.astype(BF16).reshape(CONV_WIDTH, n_lb, 1, LANES).transpose(1, 0, 2, 3),
        (n_lb, CONV_WIDTH, PACK, LANES))
    row = lambda b, s: (b * n_s + s, 0)
    fixed = lambda b, s: (0, 0)
    vec = pl.BlockSpec((1, D_MODEL), fixed)
    return pl.pallas_call(
        functools.partial(_conv_body, ts=ts),
        grid=(batch, n_s),
        in_specs=[
            pl.BlockSpec((ts, D_MODEL), row),
            vec,
            pl.BlockSpec((None, D_MODEL, 2 * D_MODEL), lambda b, s: (layer, 0, 0)),
            pl.BlockSpec((1, 2 * D_MODEL), fixed),
            pl.BlockSpec((n_lb, CONV_WIDTH, PACK, LANES), lambda b, s: (0, 0, 0, 0)),
            vec, vec, vec,
            pl.BlockSpec((None, D_MODEL, D_MODEL), lambda b, s: (layer, 0, 0)),
            vec,
        ],
        out_specs=pl.BlockSpec((ts, D_MODEL), row),
        out_shape=jax.ShapeDtypeStruct(h.shape, F32),
        scratch_shapes=[
            pltpu.VMEM((n_lb, CONV_HALO + ts, LANES), F32),
            pltpu.VMEM((n_lb, ts, LANES), F32),
            pltpu.VMEM((PACK, ts + PACK, LANES), BF16),
        ],
        compiler_params=_params(("parallel", "arbitrary")),
        name="conformer_conv",
    )(h, norm_g.reshape(1, -1), w_in, b_in.reshape(1, -1), wdw, b_dw.reshape(1, -1),
      ln_g.reshape(1, -1), ln_b.reshape(1, -1), w_out, b_out.reshape(1, -1))


def _bucket_tiles(t):
    assert t >= MAX_DISTANCE
    qp = np.arange(t)[:, None]
    kp = np.arange(t)[None, :]
    max_exact = NUM_BUCKETS // 2

    def bucket(n):
        nf = np.maximum(n, 1).astype(np.float32)
        large = max_exact + (np.log(nf / max_exact) / math.log(MAX_DISTANCE / max_exact)
                             * (NUM_BUCKETS - max_exact)).astype(np.int32)
        return np.where(n < max_exact, n, np.minimum(large, NUM_BUCKETS - 1))

    diag = np.where(kp <= qp, bucket(np.maximum(qp - kp, 0)), -1)
    left = bucket(qp + t - kp)
    return np.stack([diag, left]).astype(np.int32)


def _bias_body(rb_ref, bucket_ref, out_ref):
    h = pl.program_id(0)
    far = rb_ref[NUM_BUCKETS - 1, h]
    bk = bucket_ref[...]
    acc = jnp.full(bk.shape, NEG_INF, F32)
    for b in range(NUM_BUCKETS):
        acc = jnp.where(bk == b, rb_ref[b, h] - far, acc)
    out_ref[...] = acc


def _rel_bias_tiles(rel_bias, t):
    return pl.pallas_call(
        _bias_body,
        grid=(N_HEADS,),
        in_specs=[
            pl.BlockSpec(memory_space=pltpu.SMEM),
            pl.BlockSpec((2, t, t), lambda h: (0, 0, 0)),
        ],
        out_specs=pl.BlockSpec((None, 2, t, t), lambda h: (h, 0, 0, 0)),
        out_shape=jax.ShapeDtypeStruct((N_HEADS, 2, t, t), F32),
        compiler_params=_params(("parallel",)),
        name="rel_bias_tiles",
    )(rel_bias.astype(F32), jnp.asarray(_bucket_tiles(t)))


def _attn_body(q_ref, kt_ref, v_ref, bias_ref, lam_ref, sub_ref, out_ref, vaug_ref,
               *, t, n_q, lambda_init):
    seq = v_ref.shape[0]
    lq = lam_ref[...]
    lam = (jnp.exp(jnp.sum(lq[0:1] * lq[1:2], axis=-1, keepdims=True))
           - jnp.exp(jnp.sum(lq[2:3] * lq[3:4], axis=-1, keepdims=True)) + lambda_init)
    for hh in range(ATT_HEADS):
        vaug_ref[hh, :, :LANES] = v_ref[:, hh * LANES:(hh + 1) * LANES]
        vaug_ref[hh, :, LANES:] = jnp.ones((seq, LANES), BF16)

    for hh, i in [(hh, i) for hh in range(ATT_HEADS) for i in reversed(range(n_q))]:
        head = slice(hh * LANES, (hh + 1) * LANES)
        q = q_ref[i * t:(i + 1) * t, head]
        lane = lax.broadcasted_iota(jnp.int32, q.shape, 1)
        zero = jnp.zeros_like(q)
        q12 = jnp.concatenate([jnp.where(lane < HEAD_DIM, q, zero),
                               jnp.where(lane >= HEAD_DIM, q, zero)], axis=0)
        pieces = []
        if i >= 2:
            pieces.append((0, (i - 1) * t, None))
        if i >= 1:
            pieces.append(((i - 1) * t, i * t, 1))
        pieces.append((i * t, (i + 1) * t, 0))

        scores = []
        colmax = None
        for c0, c1, bias_idx in pieces:
            s = jnp.dot(q12, kt_ref[head, c0:c1], preferred_element_type=F32)
            if bias_idx is not None:
                s = (s.reshape(2, t, t) + bias_ref[hh, bias_idx][None]).reshape(2 * t, t)
            for c in range((c1 - c0) // LANES):
                blk = s[:, c * LANES:(c + 1) * LANES]
                colmax = blk if colmax is None else jnp.maximum(colmax, blk)
            scores.append(s)
        m = jnp.max(colmax, axis=1, keepdims=True)

        acc = None
        for (c0, c1, _), s in zip(pieces, scores):
            p = jnp.exp(s - m).astype(BF16)
            d = jnp.dot(p, vaug_ref[hh, c0:c1, :], preferred_element_type=F32)
            acc = d if acc is None else acc + d
        o12 = acc[:, :LANES] / acc[:, LANES:]
        o = o12[:t] - lam * o12[t:]
        out_ref[i * t:(i + 1) * t, head] = (
            _rms(o, sub_ref[...]) * (1.0 - lambda_init)).astype(out_ref.dtype)


def _attention(q, kt, v, bias_tiles, lam_rows, subln, batch, seq, lambda_init):
    t = ATT_T
    n_q = seq // t
    q3 = q.reshape(batch, seq, QK_DIM)
    width = ATT_HEADS * LANES
    head_cols = pl.BlockSpec((None, seq, width), lambda b, h: (b, 0, h))
    out = pl.pallas_call(
        functools.partial(_attn_body, t=t, n_q=n_q, lambda_init=lambda_init),
        grid=(batch, N_HEADS // ATT_HEADS),
        in_specs=[
            head_cols,
            pl.BlockSpec((None, width, seq), lambda b, h: (b, h, 0)),
            head_cols,
            pl.BlockSpec((ATT_HEADS, 2, t, t), lambda b, h: (h, 0, 0, 0)),
            pl.BlockSpec((4, HEAD_DIM), lambda b, h: (0, 0)),
            pl.BlockSpec((1, LANES), lambda b, h: (0, 0)),
        ],
        out_specs=head_cols,
        out_shape=jax.ShapeDtypeStruct((batch, seq, V_DIM), BF16),
        scratch_shapes=[pltpu.VMEM((ATT_HEADS, seq, 2 * LANES), BF16)],
        compiler_params=_params(("parallel", "parallel")),
        name="diff_attention",
    )(q3, kt, v, bias_tiles, lam_rows, subln.reshape(1, LANES))
    return out.reshape(batch * seq, V_DIM)


def _kv_body(h_ref, g_ref, wkt_ref, wv_ref, kt_ref, v_ref):
    xn = _rms(h_ref[...], g_ref[...]).astype(BF16)
    kt = lax.dot_general(wkt_ref[...], xn, (((1,), (1,)), ((), ())),
                         preferred_element_type=F32)
    kt_ref[...] = kt.astype(BF16)
    v_ref[...] = jnp.dot(xn, wv_ref[...], preferred_element_type=F32).astype(BF16)


def _shared_kv(h, batch, seq, norm_g, w_kt, w_v):
    tm = PROJ_TM
    n_s = seq // tm
    fixed = lambda b, s: (0, 0)
    return pl.pallas_call(
        _kv_body,
        grid=(batch, n_s),
        in_specs=[
            pl.BlockSpec((tm, D_MODEL), lambda b, s: (b * n_s + s, 0)),
            pl.BlockSpec((1, D_MODEL), fixed),
            pl.BlockSpec((QK_DIM, D_MODEL), fixed),
            pl.BlockSpec((D_MODEL, V_DIM), fixed),
        ],
        out_specs=[
            pl.BlockSpec((None, QK_DIM, tm), lambda b, s: (b, 0, s)),
            pl.BlockSpec((None, tm, V_DIM), lambda b, s: (b, s, 0)),
        ],
        out_shape=[
            jax.ShapeDtypeStruct((batch, QK_DIM, seq), BF16),
            jax.ShapeDtypeStruct((batch, seq, V_DIM), BF16),
        ],
        compiler_params=_params(("parallel", "parallel")),
        name="shared_kv",
    )(h, norm_g.reshape(1, D_MODEL), w_kt, w_v)


def _head_major(w):
    lead = w.shape[:-1]
    w = w.reshape(*lead, 2, N_HEADS, HEAD_DIM)
    return jnp.swapaxes(w, -3, -2).reshape(*lead, QK_DIM)


def kernel(x, p, ffn1_norm, ffn1_w_in, ffn1_w_out, mix_norm, ffn2_norm, ffn2_w_in, ffn2_w_out,
           ple_norm, ple_w_gate, ple_w_proj, conv_w_in, conv_b_in, conv_w_dw, conv_b_dw,
           conv_ln_g, conv_ln_b, conv_w_out, conv_b_out, kv_norm, w_kv, attn_w_q,
           attn_lq1, attn_lk1, attn_lq2, attn_lk2, attn_subln, attn_w_o, rel_bias, final_norm):
    batch, seq, d_model = x.shape
    tokens = batch * seq
    assert d_model == D_MODEL and p.shape == (DEPTH, batch, seq, PLE_DIM)
    assert tokens % FFN_TM == 0 and seq % CONV_TS == 0 and seq % PROJ_TM == 0 and seq % ATT_T == 0
    h = x.reshape(tokens, D_MODEL)
    p2 = p.reshape(DEPTH, tokens, PLE_DIM)
    bias_tiles = _rel_bias_tiles(rel_bias, ATT_T)
    w_kt = _head_major(w_kv[:, :QK_DIM]).T.astype(BF16)
    w_v = w_kv[:, QK_DIM:].astype(BF16)
    ple_w_gate, ple_w_proj = ple_w_gate.astype(BF16), ple_w_proj.astype(BF16)
    conv_w_in, conv_w_out = conv_w_in.astype(BF16), conv_w_out.astype(BF16)
    attn_w_o = attn_w_o.astype(BF16)
    w_q = (_head_major(attn_w_q) * (HEAD_DIM ** -0.5)).astype(BF16)
    w_in, w_out = ffn1_w_in[0:1].astype(BF16), ffn1_w_out[0:1].astype(BF16)
    kt = v = o = None
    for i in range(DEPTH):
        j = i - N_A
        if i == N_A:
            kt, v = _shared_kv(h, batch, seq, kv_norm, w_kt, w_v)
        res = _ffn(h, 0, ffn1_norm[i], w_in, w_out,
                   q_out=(mix_norm[i], j, w_q) if i >= N_A else None,
                   cast_next=(ffn2_w_in, ffn2_w_out, i))
        h, w_in, w_out = res[0], res[-2][None], res[-1][None]
        if i < N_A:
            h = _conv_layer(h, batch, seq, i, mix_norm[i], conv_w_in, conv_b_in[i],
                            conv_w_dw[i], conv_b_dw[i], conv_ln_g[i], conv_ln_b[i],
                            conv_w_out, conv_b_out[i])
        else:
            lambda_init = 0.8 - 0.6 * math.exp(-0.3 * i)
            lam_rows = jnp.stack([attn_lq1[j], attn_lk1[j], attn_lq2[j], attn_lk2[j]]).astype(F32)
            o = _attention(res[1], kt, v, bias_tiles, lam_rows, attn_subln[j], batch, seq,
                           lambda_init)
        last = i == DEPTH - 1
        res = _ffn(h, 0, ffn2_norm[i], w_in, w_out,
                   attn_in=(o, j, attn_w_o) if i >= N_A else None,
                   ple=(ple_norm[i], i, ple_w_gate, p2, ple_w_proj),
                   final_g=final_norm if last else None,
                   cast_next=None if last else (ffn1_w_in, ffn1_w_out, i + 1))
        h = res[0]
        if not last:
            w_in, w_out = res[-2][None], res[-1][None]
    return h.reshape(batch, seq, D_MODEL)
```

```python
import functools
import math

import jax
import jax.numpy as jnp
import numpy as np
from jax import lax
from jax.experimental import pallas as pl
from jax.experimental.pallas import tpu as pltpu

D_MODEL = 1024
DEPTH = 4
N_A = DEPTH // 2
D_FF = 4 * D_MODEL
CONV_WIDTH = 31
HEAD_DIM = 64
N_HEADS = D_MODEL // (2 * HEAD_DIM)
QK_DIM = 2 * N_HEADS * HEAD_DIM
V_DIM = N_HEADS * 2 * HEAD_DIM
NUM_BUCKETS = 32
MAX_DISTANCE = 128
PLE_DIM = 256
RMS_EPS = 1e-6
LN_EPS = 1e-5
NEG_INF = -1e30

LANES = 128
PACK = 16
VMEM_LIMIT_BYTES = 56 * 1024 * 1024

FFN_TM = 1024
FFN_TF = 1024
FFN_TF_BARE = 2048
FFN_SUB = 256
PROJ_TM = 1024
CONV_TS = 1024
CONV_HALO = 32
ATT_T = 256
ATT_HEADS = 2

F32 = jnp.float32
BF16 = jnp.bfloat16


def _rms(x, g):
    ms = jnp.mean(x * x, axis=-1, keepdims=True)
    return x * lax.rsqrt(ms + RMS_EPS) * g


def _sigmoid(x):
    return 1.0 / (1.0 + jnp.exp(-x))


def _params(sem):
    return pltpu.CompilerParams(dimension_semantics=sem, vmem_limit_bytes=VMEM_LIMIT_BYTES)


def _ffn_body(*refs, n_f, attn_in, ple, final, q_out, cast):
    it = iter(refs)
    h_ref, g_ref, wg_ref, wu_ref, wo_ref = (next(it) for _ in range(5))
    if attn_in:
        o_ref, wproj_ref = next(it), next(it)
    if ple:
        pn_ref, pwg_ref, p_ref, pwp_ref = (next(it) for _ in range(4))
    if final:
        fn_ref = next(it)
    if q_out:
        qg_ref, wq_ref = next(it), next(it)
    if cast:
        nwin_ref, nwout_ref = next(it), next(it)
    out_ref = next(it)
    if q_out:
        q_ref = next(it)
    if cast:
        cwin_ref, cwout_ref = next(it), next(it)
    xn_ref = next(it)

    f = pl.program_id(1)

    if cast:
        cwin_ref[...] = nwin_ref[...].astype(BF16)
        cwout_ref[...] = nwout_ref[...].astype(BF16)

    def step(first):
        if first:
            x = h_ref[...]
            if attn_in:
                x = x + jnp.dot(o_ref[...], wproj_ref[...], preferred_element_type=F32)
                out_ref[...] = x
            xn = _rms(x, g_ref[...]).astype(BF16)
            xn_ref[...] = xn
        else:
            xn = xn_ref[...]
        part = None
        for c in range(wg_ref.shape[1] // FFN_SUB):
            cols = slice(c * FFN_SUB, (c + 1) * FFN_SUB)
            gate = jnp.dot(xn, wg_ref[:, cols], preferred_element_type=F32)
            up = jnp.dot(xn, wu_ref[:, cols], preferred_element_type=F32)
            act = (gate * _sigmoid(gate) * up).astype(BF16)
            d = jnp.dot(act, wo_ref[cols, :], preferred_element_type=F32)
            part = d if part is None else part + d
        if first and not attn_in:
            out_ref[...] = h_ref[...] + 0.5 * part
        else:
            out_ref[...] += 0.5 * part

    pl.when(f == 0)(functools.partial(step, True))
    pl.when(f > 0)(functools.partial(step, False))

    @pl.when(f == n_f - 1)
    def _():
        hn = out_ref[...]
        if ple:
            xg = _rms(hn, pn_ref[...]).astype(BF16)
            gt = _sigmoid(jnp.dot(xg, pwg_ref[...], preferred_element_type=F32))
            pr = jnp.dot(p_ref[...].astype(BF16), pwp_ref[...], preferred_element_type=F32)
            hn = hn + gt * pr
        if final:
            hn = _rms(hn, fn_ref[...])
        out_ref[...] = hn
        if q_out:
            xq = _rms(hn, qg_ref[...]).astype(BF16)
            q_ref[...] = jnp.dot(xq, wq_ref[...], preferred_element_type=F32).astype(BF16)


def _ffn(h, layer, norm_g, w_in, w_out, attn_in=None, ple=None, final_g=None, q_out=None,
         cast_next=None):
    t = h.shape[0]
    tm = FFN_TM
    bare = attn_in is None and ple is None and q_out is None
    tf = FFN_TF_BARE if bare else FFN_TF
    n_t, n_f = t // tm, D_FF // tf
    row = lambda i, f: (i, 0)
    fixed = lambda i, f: (0, 0)
    once = pl.Buffered(1)
    vec = pl.BlockSpec((1, D_MODEL), fixed, pipeline_mode=once)

    def square(lyr):
        return pl.BlockSpec((None, D_MODEL, D_MODEL), lambda i, f: (lyr, 0, 0), pipeline_mode=once)

    in_specs = [
        pl.BlockSpec((tm, D_MODEL), row),
        vec,
        pl.BlockSpec((None, D_MODEL, tf), lambda i, f: (layer, 0, f)),
        pl.BlockSpec((None, D_MODEL, tf), lambda i, f: (layer, 0, f + n_f)),
        pl.BlockSpec((None, tf, D_MODEL), lambda i, f: (layer, f, 0)),
    ]
    args = [h, norm_g.reshape(1, D_MODEL), w_in, w_in, w_out]
    if attn_in is not None:
        o, proj_layer, w_proj = attn_in
        in_specs += [pl.BlockSpec((tm, V_DIM), row), square(proj_layer)]
        args += [o, w_proj]
    if ple is not None:
        pn, ple_layer, pwg, p, pwp = ple
        in_specs += [
            vec,
            square(ple_layer),
            pl.BlockSpec((None, tm, PLE_DIM), lambda i, f: (ple_layer, i, 0)),
            pl.BlockSpec((None, PLE_DIM, D_MODEL), lambda i, f: (ple_layer, 0, 0),
                         pipeline_mode=once),
        ]
        args += [pn.reshape(1, D_MODEL), pwg, p, pwp]
    if final_g is not None:
        in_specs.append(vec)
        args.append(final_g.reshape(1, D_MODEL))
    if q_out is not None:
        g_q, q_layer, w_q = q_out
        in_specs += [vec, square(q_layer)]
        args += [g_q.reshape(1, D_MODEL), w_q]
    out_specs = [pl.BlockSpec((tm, D_MODEL), row)]
    out_shape = [jax.ShapeDtypeStruct((t, D_MODEL), F32)]
    if q_out is not None:
        out_specs.append(pl.BlockSpec((tm, QK_DIM), row))
        out_shape.append(jax.ShapeDtypeStruct((t, QK_DIM), BF16))
    if cast_next is not None:
        nw_in, nw_out, n_layer = cast_next
        steps = n_t * n_f
        r_in, r_out = D_MODEL // steps, D_FF // steps
        slab = lambda i, f: (i * n_f + f, 0)
        in_specs += [
            pl.BlockSpec((None, r_in, 2 * D_FF), lambda i, f: (n_layer, i * n_f + f, 0)),
            pl.BlockSpec((None, r_out, D_MODEL), lambda i, f: (n_layer, i * n_f + f, 0)),
        ]
        args += [nw_in, nw_out]
        out_specs += [pl.BlockSpec((r_in, 2 * D_FF), slab), pl.BlockSpec((r_out, D_MODEL), slab)]
        out_shape += [jax.ShapeDtypeStruct((D_MODEL, 2 * D_FF), BF16),
                      jax.ShapeDtypeStruct((D_FF, D_MODEL), BF16)]
    body = functools.partial(_ffn_body, n_f=n_f, attn_in=attn_in is not None, ple=ple is not None,
                             final=final_g is not None, q_out=q_out is not None,
                             cast=cast_next is not None)
    return pl.pallas_call(
        body,
        grid=(n_t, n_f),
        in_specs=in_specs,
        out_specs=out_specs,
        out_shape=out_shape,
        scratch_shapes=[pltpu.VMEM((tm, D_MODEL), BF16)],
        compiler_params=_params(("parallel", "arbitrary")),
        name="ffn",
    )(*args)


def _conv_body(h_ref, g_ref, win_ref, bin_ref, wdw_ref, bdw_ref, lng_ref, lnb_ref,
               wout_ref, bout_ref, out_ref, ubuf_ref, cbuf_ref, sh_ref, *, ts):
    n_lb = D_MODEL // LANES
    halo = CONV_HALO
    first_tap_row = halo - (CONV_WIDTH - 1)

    @pl.when(pl.program_id(1) == 0)
    def _():
        ubuf_ref[:, 0:halo, :] = jnp.zeros((n_lb, halo, LANES), F32)

    x = h_ref[...]
    xn = _rms(x, g_ref[...]).astype(BF16)
    y = jnp.dot(xn, win_ref[...], preferred_element_type=F32) + bin_ref[...]
    u = y[:, :D_MODEL] * _sigmoid(y[:, D_MODEL:])
    for j in range(n_lb):
        ubuf_ref[j, halo:halo + ts, :] = u[:, j * LANES:(j + 1) * LANES]

    def lane_block(j, carry):
        for r in range(PACK):
            n = ts + (PACK if r < PACK - 1 else 0)
            sh_ref[r, 0:n, :] = ubuf_ref[j, pl.ds(first_tap_row + r, n), :].astype(BF16)
        for g in range(ts // PACK):
            acc = None
            for k in range(CONV_WIDTH):
                a, r = divmod(k, PACK)
                seg = sh_ref[r, (g + a) * PACK:(g + a + 1) * PACK, :]
                term = seg.astype(F32) * wdw_ref[j, k].astype(F32)
                acc = term if acc is None else acc + term
            cbuf_ref[j, g * PACK:(g + 1) * PACK, :] = acc
        return carry

    lax.fori_loop(0, n_lb, lane_block, 0)

    ubuf_ref[:, 0:halo, :] = ubuf_ref[:, ts:ts + halo, :]

    c = jnp.concatenate([cbuf_ref[j] for j in range(n_lb)], axis=1) + bdw_ref[...]
    mu = jnp.mean(c, axis=-1, keepdims=True)
    d = c - mu
    var = jnp.mean(d * d, axis=-1, keepdims=True)
    z = d * lax.rsqrt(var + LN_EPS) * lng_ref[...] + lnb_ref[...]
    z = (z * _sigmoid(z)).astype(BF16)
    out_ref[...] = x + jnp.dot(z, wout_ref[...], preferred_element_type=F32) + bout_ref[...]


def _conv_layer(h, batch, seq, layer, norm_g, w_in, b_in, w_dw, b_dw, ln_g, ln_b, w_out, b_out):
    ts = CONV_TS
    n_s = seq // ts
    n_lb = D_MODEL // LANES
    wdw = jnp.broadcast_to(
        w_dw.astype(BF16).reshape(CONV_WIDTH, n_lb, 1, LANES).transpose(1, 0, 2, 3),
        (n_lb, CONV_WIDTH, PACK, LANES))
    row = lambda b, s: (b * n_s + s, 0)
    fixed = lambda b, s: (0, 0)
    vec = pl.BlockSpec((1, D_MODEL), fixed)
    return pl.pallas_call(
        functools.partial(_conv_body, ts=ts),
        grid=(batch, n_s),
        in_specs=[
            pl.BlockSpec((ts, D_MODEL), row),
            vec,
            pl.BlockSpec((None, D_MODEL, 2 * D_MODEL), lambda b, s: (layer, 0, 0)),
            pl.BlockSpec((1, 2 * D_MODEL), fixed),
            pl.BlockSpec((n_lb, CONV_WIDTH, PACK, LANES), lambda b, s: (0, 0, 0, 0)),
            vec, vec, vec,
            pl.BlockSpec((None, D_MODEL, D_MODEL), lambda b, s: (layer, 0, 0)),
            vec,
        ],
        out_specs=pl.BlockSpec((ts, D_MODEL), row),
        out_shape=jax.ShapeDtypeStruct(h.shape, F32),
        scratch_shapes=[
            pltpu.VMEM((n_lb, CONV_HALO + ts, LANES), F32),
            pltpu.VMEM((n_lb, ts, LANES), F32),
            pltpu.VMEM((PACK, ts + PACK, LANES), BF16),
        ],
        compiler_params=_params(("parallel", "arbitrary")),
        name="conformer_conv",
    )(h, norm_g.reshape(1, -1), w_in, b_in.reshape(1, -1), wdw, b_dw.reshape(1, -1),
      ln_g.reshape(1, -1), ln_b.reshape(1, -1), w_out, b_out.reshape(1, -1))


def _bucket_tiles(t):
    assert t >= MAX_DISTANCE
    qp = np.arange(t)[:, None]
    kp = np.arange(t)[None, :]
    max_exact = NUM_BUCKETS // 2

    def bucket(n):
        nf = np.maximum(n, 1).astype(np.float32)
        large = max_exact + (np.log(nf / max_exact) / math.log(MAX_DISTANCE / max_exact)
                             * (NUM_BUCKETS - max_exact)).astype(np.int32)
        return np.where(n < max_exact, n, np.minimum(large, NUM_BUCKETS - 1))

    diag = np.where(kp <= qp, bucket(np.maximum(qp - kp, 0)), -1)
    left = bucket(qp + t - kp)
    return np.stack([diag, left]).astype(np.int32)


def _bias_body(rb_ref, bucket_ref, out_ref):
    h = pl.program_id(0)
    far = rb_ref[NUM_BUCKETS - 1, h]
    bk = bucket_ref[...]
    acc = jnp.full(bk.shape, NEG_INF, F32)
    for b in range(NUM_BUCKETS):
        acc = jnp.where(bk == b, rb_ref[b, h] - far, acc)
    out_ref[...] = acc


def _rel_bias_tiles(rel_bias, t):
    return pl.pallas_call(
        _bias_body,
        grid=(N_HEADS,),
        in_specs=[
            pl.BlockSpec(memory_space=pltpu.SMEM),
            pl.BlockSpec((2, t, t), lambda h: (0, 0, 0)),
        ],
        out_specs=pl.BlockSpec((None, 2, t, t), lambda h: (h, 0, 0, 0)),
        out_shape=jax.ShapeDtypeStruct((N_HEADS, 2, t, t), F32),
        compiler_params=_params(("parallel",)),
        name="rel_bias_tiles",
    )(rel_bias.astype(F32), jnp.asarray(_bucket_tiles(t)))


def _attn_body(q_ref, kt_ref, v_ref, bias_ref, lam_ref, sub_ref, out_ref, vaug_ref,
               *, t, n_q, lambda_init):
    seq = v_ref.shape[0]
    lq = lam_ref[...]
    lam = (jnp.exp(jnp.sum(lq[0:1] * lq[1:2], axis=-1, keepdims=True))
           - jnp.exp(jnp.sum(lq[2:3] * lq[3:4], axis=-1, keepdims=True)) + lambda_init)
    for hh in range(ATT_HEADS):
        vaug_ref[hh, :, :LANES] = v_ref[:, hh * LANES:(hh + 1) * LANES]
        vaug_ref[hh, :, LANES:] = jnp.ones((seq, LANES), BF16)

    for hh, i in [(hh, i) for hh in range(ATT_HEADS) for i in reversed(range(n_q))]:
        head = slice(hh * LANES, (hh + 1) * LANES)
        q = q_ref[i * t:(i + 1) * t, head]
        lane = lax.broadcasted_iota(jnp.int32, q.shape, 1)
        zero = jnp.zeros_like(q)
        q12 = jnp.concatenate([jnp.where(lane < HEAD_DIM, q, zero),
                               jnp.where(lane >= HEAD_DIM, q, zero)], axis=0)
        pieces = []
        if i >= 2:
            pieces.append((0, (i - 1) * t, None))
        if i >= 1:
            pieces.append(((i - 1) * t, i * t, 1))
        pieces.append((i * t, (i + 1) * t, 0))

        scores = []
        colmax = None
        for c0, c1, bias_idx in pieces:
            s = jnp.dot(q12, kt_ref[head, c0:c1], preferred_element_type=F32)
            if bias_idx is not None:
                s = (s.reshape(2, t, t) + bias_ref[hh, bias_idx][None]).reshape(2 * t, t)
            for c in range((c1 - c0) // LANES):
                blk = s[:, c * LANES:(c + 1) * LANES]
                colmax = blk if colmax is None else jnp.maximum(colmax, blk)
            scores.append(s)
        m = jnp.max(colmax, axis=1, keepdims=True)

        acc = None
        for (c0, c1, _), s in zip(pieces, scores):
            p = jnp.exp(s - m).astype(BF16)
            d = jnp.dot(p, vaug_ref[hh, c0:c1, :], preferred_element_type=F32)
            acc = d if acc is None else acc + d
        o12 = acc[:, :LANES] / acc[:, LANES:]
        o = o12[:t] - lam * o12[t:]
        out_ref[i * t:(i + 1) * t, head] = (
            _rms(o, sub_ref[...]) * (1.0 - lambda_init)).astype(out_ref.dtype)


def _attention(q, kt, v, bias_tiles, lam_rows, subln, batch, seq, lambda_init):
    t = ATT_T
    n_q = seq // t
    q3 = q.reshape(batch, seq, QK_DIM)
    width = ATT_HEADS * LANES
    head_cols = pl.BlockSpec((None, seq, width), lambda b, h: (b, 0, h))
    out = pl.pallas_call(
        functools.partial(_attn_body, t=t, n_q=n_q, lambda_init=lambda_init),
        grid=(batch, N_HEADS // ATT_HEADS),
        in_specs=[
            head_cols,
            pl.BlockSpec((None, width, seq), lambda b, h: (b, h, 0)),
            head_cols,
            pl.BlockSpec((ATT_HEADS, 2, t, t), lambda b, h: (h, 0, 0, 0)),
            pl.BlockSpec((4, HEAD_DIM), lambda b, h: (0, 0)),
            pl.BlockSpec((1, LANES), lambda b, h: (0, 0)),
        ],
        out_specs=head_cols,
        out_shape=jax.ShapeDtypeStruct((batch, seq, V_DIM), BF16),
        scratch_shapes=[pltpu.VMEM((ATT_HEADS, seq, 2 * LANES), BF16)],
        compiler_params=_params(("parallel", "parallel")),
        name="diff_attention",
    )(q3, kt, v, bias_tiles, lam_rows, subln.reshape(1, LANES))
    return out.reshape(batch * seq, V_DIM)


def _kv_body(h_ref, g_ref, wkt_ref, wv_ref, kt_ref, v_ref):
    xn = _rms(h_ref[...], g_ref[...]).astype(BF16)
    kt = lax.dot_general(wkt_ref[...], xn, (((1,), (1,)), ((), ())),
                         preferred_element_type=F32)
    kt_ref[...] = kt.astype(BF16)
    v_ref[...] = jnp.dot(xn, wv_ref[...], preferred_element_type=F32).astype(BF16)


def _shared_kv(h, batch, seq, norm_g, w_kt, w_v):
    tm = PROJ_TM
    n_s = seq // tm
    fixed = lambda b, s: (0, 0)
    return pl.pallas_call(
        _kv_body,
        grid=(batch, n_s),
        in_specs=[
            pl.BlockSpec((tm, D_MODEL), lambda b, s: (b * n_s + s, 0)),
            pl.BlockSpec((1, D_MODEL), fixed),
            pl.BlockSpec((QK_DIM, D_MODEL), fixed),
            pl.BlockSpec((D_MODEL, V_DIM), fixed),
        ],
        out_specs=[
            pl.BlockSpec((None, QK_DIM, tm), lambda b, s: (b, 0, s)),
            pl.BlockSpec((None, tm, V_DIM), lambda b, s: (b, s, 0)),
        ],
        out_shape=[
            jax.ShapeDtypeStruct((batch, QK_DIM, seq), BF16),
            jax.ShapeDtypeStruct((batch, seq, V_DIM), BF16),
        ],
        compiler_params=_params(("parallel", "parallel")),
        name="shared_kv",
    )(h, norm_g.reshape(1, D_MODEL), w_kt, w_v)


def _head_major(w):
    lead = w.shape[:-1]
    w = w.reshape(*lead, 2, N_HEADS, HEAD_DIM)
    return jnp.swapaxes(w, -3, -2).reshape(*lead, QK_DIM)


def kernel(x, p, ffn1_norm, ffn1_w_in, ffn1_w_out, mix_norm, ffn2_norm, ffn2_w_in, ffn2_w_out,
           ple_norm, ple_w_gate, ple_w_proj, conv_w_in, conv_b_in, conv_w_dw, conv_b_dw,
           conv_ln_g, conv_ln_b, conv_w_out, conv_b_out, kv_norm, w_kv, attn_w_q,
           attn_lq1, attn_lk1, attn_lq2, attn_lk2, attn_subln, attn_w_o, rel_bias, final_norm):
    batch, seq, d_model = x.shape
    tokens = batch * seq
    assert d_model == D_MODEL and p.shape == (DEPTH, batch, seq, PLE_DIM)
    assert tokens % FFN_TM == 0 and seq % CONV_TS == 0 and seq % PROJ_TM == 0 and seq % ATT_T == 0
    h = x.reshape(tokens, D_MODEL)
    p2 = p.reshape(DEPTH, tokens, PLE_DIM)
    bias_tiles = _rel_bias_tiles(rel_bias, ATT_T)
    w_kt = _head_major(w_kv[:, :QK_DIM]).T.astype(BF16)
    w_v = w_kv[:, QK_DIM:].astype(BF16)
    ple_w_gate, ple_w_proj = ple_w_gate.astype(BF16), ple_w_proj.astype(BF16)
    conv_w_in, conv_w_out = conv_w_in.astype(BF16), conv_w_out.astype(BF16)
    attn_w_o = attn_w_o.astype(BF16)
    w_q = (_head_major(attn_w_q) * (HEAD_DIM ** -0.5)).astype(BF16)
    w_in, w_out = ffn1_w_in[0:1].astype(BF16), ffn1_w_out[0:1].astype(BF16)
    kt = v = o = None
    for i in range(DEPTH):
        j = i - N_A
        if i == N_A:
            kt, v = _shared_kv(h, batch, seq, kv_norm, w_kt, w_v)
        res = _ffn(h, 0, ffn1_norm[i], w_in, w_out,
                   q_out=(mix_norm[i], j, w_q) if i >= N_A else None,
                   cast_next=(ffn2_w_in, ffn2_w_out, i))
        h, w_in, w_out = res[0], res[-2][None], res[-1][None]
        if i < N_A:
            h = _conv_layer(h, batch, seq, i, mix_norm[i], conv_w_in, conv_b_in[i],
                            conv_w_dw[i], conv_b_dw[i], conv_ln_g[i], conv_ln_b[i],
                            conv_w_out, conv_b_out[i])
        else:
            lambda_init = 0.8 - 0.6 * math.exp(-0.3 * i)
            lam_rows = jnp.stack([attn_lq1[j], attn_lk1[j], attn_lq2[j], attn_lk2[j]]).astype(F32)
            o = _attention(res[1], kt, v, bias_tiles, lam_rows, attn_subln[j], batch, seq,
                           lambda_init)
        last = i == DEPTH - 1
        res = _ffn(h, 0, ffn2_norm[i], w_in, w_out,
                   attn_in=(o, j, attn_w_o) if i >= N_A else None,
                   ple=(ple_norm[i], i, ple_w_gate, p2, ple_w_proj),
                   final_g=final_norm if last else None,
                   cast_next=None if last else (ffn1_w_in, ffn1_w_out, i + 1))
        h = res[0]
        if not last:
            w_in, w_out = res[-2][None], res[-1][None]
    return h.reshape(batch, seq, D_MODEL)
```

```python
import functools
import math

import jax
import jax.numpy as jnp
import numpy as np
from jax import lax
from jax.experimental import pallas as pl
from jax.experimental.pallas import tpu as pltpu

D_MODEL = 1024
DEPTH = 4
N_A = DEPTH // 2
D_FF = 4 * D_MODEL
CONV_WIDTH = 31
HEAD_DIM = 64
N_HEADS = D_MODEL // (2 * HEAD_DIM)
QK_DIM = 2 * N_HEADS * HEAD_DIM
V_DIM = N_HEADS * 2 * HEAD_DIM
NUM_BUCKETS = 32
MAX_DISTANCE = 128
PLE_DIM = 256
RMS_EPS = 1e-6
LN_EPS = 1e-5
NEG_INF = -1e30

LANES = 128
PACK = 16
VMEM_LIMIT_BYTES = 56 * 1024 * 1024

FFN_TM = 1024
FFN_TF = 1024
FFN_TF_BARE = 2048
FFN_SUB = 256
PROJ_TM = 2048
CAST_STEPS = 8
CONV_TS = 1024
CONV_HALO = 32
ATT_T = 256
ATT_HEADS = 2

F32 = jnp.float32
BF16 = jnp.bfloat16


def _rms(x, g):
    ms = jnp.mean(x * x, axis=-1, keepdims=True)
    return x * lax.rsqrt(ms + RMS_EPS) * g


def _sigmoid(x):
    return 1.0 / (1.0 + jnp.exp(-x))


def _params(sem):
    return pltpu.CompilerParams(dimension_semantics=sem, vmem_limit_bytes=VMEM_LIMIT_BYTES)


def _ffn_body(*refs, n_f, attn_in, ple, final, q_out, cast):
    it = iter(refs)
    h_ref, g_ref, wg_ref, wu_ref, wo_ref = (next(it) for _ in range(5))
    if attn_in:
        o_ref, wproj_ref = next(it), next(it)
    if ple:
        pn_ref, pwg_ref, p_ref, pwp_ref = (next(it) for _ in range(4))
    if final:
        fn_ref = next(it)
    if q_out:
        qg_ref, wq_ref = next(it), next(it)
    if cast:
        nwin_ref, nwout_ref = next(it), next(it)
    out_ref = next(it)
    if q_out:
        q_ref = next(it)
    if cast:
        cwin_ref, cwout_ref = next(it), next(it)
    xn_ref = next(it)

    f = pl.program_id(1)

    if cast:
        cwin_ref[...] = nwin_ref[...].astype(BF16)
        cwout_ref[...] = nwout_ref[...].astype(BF16)

    def step(first):
        if first:
            x = h_ref[...]
            if attn_in:
                x = x + jnp.dot(o_ref[...], wproj_ref[...], preferred_element_type=F32)
                out_ref[...] = x
            xn = _rms(x, g_ref[...]).astype(BF16)
            xn_ref[...] = xn
        else:
            xn = xn_ref[...]
        part = None
        for c in range(wg_ref.shape[1] // FFN_SUB):
            cols = slice(c * FFN_SUB, (c + 1) * FFN_SUB)
            gate = jnp.dot(xn, wg_ref[:, cols], preferred_element_type=F32)
            up = jnp.dot(xn, wu_ref[:, cols], preferred_element_type=F32)
            act = (gate * _sigmoid(gate) * up).astype(BF16)
            d = jnp.dot(act, wo_ref[cols, :], preferred_element_type=F32)
            part = d if part is None else part + d
        if first and not attn_in:
            out_ref[...] = h_ref[...] + 0.5 * part
        else:
            out_ref[...] += 0.5 * part

    pl.when(f == 0)(functools.partial(step, True))
    pl.when(f > 0)(functools.partial(step, False))

    @pl.when(f == n_f - 1)
    def _():
        hn = out_ref[...]
        if ple:
            xg = _rms(hn, pn_ref[...]).astype(BF16)
            gt = _sigmoid(jnp.dot(xg, pwg_ref[...], preferred_element_type=F32))
            pr = jnp.dot(p_ref[...].astype(BF16), pwp_ref[...], preferred_element_type=F32)
            hn = hn + gt * pr
        if final:
            hn = _rms(hn, fn_ref[...])
        out_ref[...] = hn
        if q_out:
            xq = _rms(hn, qg_ref[...]).astype(BF16)
            q_ref[...] = jnp.dot(xq, wq_ref[...], preferred_element_type=F32).astype(BF16)


def _ffn(h, layer, norm_g, w_in, w_out, attn_in=None, ple=None, final_g=None, q_out=None,
         cast_next=None):
    t = h.shape[0]
    tm = FFN_TM
    bare = attn_in is None and ple is None and q_out is None
    tf = FFN_TF_BARE if bare else FFN_TF
    n_t, n_f = t // tm, D_FF // tf
    row = lambda i, f: (i, 0)
    fixed = lambda i, f: (0, 0)
    once = pl.Buffered(1)
    vec = pl.BlockSpec((1, D_MODEL), fixed, pipeline_mode=once)

    def square(lyr):
        return pl.BlockSpec((None, D_MODEL, D_MODEL), lambda i, f: (lyr, 0, 0), pipeline_mode=once)

    in_specs = [
        pl.BlockSpec((tm, D_MODEL), row),
        vec,
        pl.BlockSpec((None, D_MODEL, tf), lambda i, f: (layer, 0, f)),
        pl.BlockSpec((None, D_MODEL, tf), lambda i, f: (layer, 0, f + n_f)),
        pl.BlockSpec((None, tf, D_MODEL), lambda i, f: (layer, f, 0)),
    ]
    args = [h, norm_g.reshape(1, D_MODEL), w_in, w_in, w_out]
    if attn_in is not None:
        o, proj_layer, w_proj = attn_in
        in_specs += [pl.BlockSpec((tm, V_DIM), row), square(proj_layer)]
        args += [o, w_proj]
    if ple is not None:
        pn, ple_layer, pwg, p, pwp = ple
        in_specs += [
            vec,
            square(ple_layer),
            pl.BlockSpec((None, tm, PLE_DIM), lambda i, f: (ple_layer, i, 0)),
            pl.BlockSpec((None, PLE_DIM, D_MODEL), lambda i, f: (ple_layer, 0, 0),
                         pipeline_mode=once),
        ]
        args += [pn.reshape(1, D_MODEL), pwg, p, pwp]
    if final_g is not None:
        in_specs.append(vec)
        args.append(final_g.reshape(1, D_MODEL))
    if q_out is not None:
        g_q, q_layer, w_q = q_out
        in_specs += [vec, square(q_layer)]
        args += [g_q.reshape(1, D_MODEL), w_q]
    out_specs = [pl.BlockSpec((tm, D_MODEL), row)]
    out_shape = [jax.ShapeDtypeStruct((t, D_MODEL), F32)]
    if q_out is not None:
        out_specs.append(pl.BlockSpec((tm, QK_DIM), row))
        out_shape.append(jax.ShapeDtypeStruct((t, QK_DIM), BF16))
    if cast_next is not None:
        nw_in, nw_out, n_layer = cast_next
        steps = n_t * n_f
        r_in, r_out = D_MODEL // steps, D_FF // steps
        slab = lambda i, f: (i * n_f + f, 0)
        in_specs += [
            pl.BlockSpec((None, r_in, 2 * D_FF), lambda i, f: (n_layer, i * n_f + f, 0)),
            pl.BlockSpec((None, r_out, D_MODEL), lambda i, f: (n_layer, i * n_f + f, 0)),
        ]
        args += [nw_in, nw_out]
        out_specs += [pl.BlockSpec((r_in, 2 * D_FF), slab), pl.BlockSpec((r_out, D_MODEL), slab)]
        out_shape += [jax.ShapeDtypeStruct((D_MODEL, 2 * D_FF), BF16),
                      jax.ShapeDtypeStruct((D_FF, D_MODEL), BF16)]
    body = functools.partial(_ffn_body, n_f=n_f, attn_in=attn_in is not None, ple=ple is not None,
                             final=final_g is not None, q_out=q_out is not None,
                             cast=cast_next is not None)
    return pl.pallas_call(
        body,
        grid=(n_t, n_f),
        in_specs=in_specs,
        out_specs=out_specs,
        out_shape=out_shape,
        scratch_shapes=[pltpu.VMEM((tm, D_MODEL), BF16)],
        compiler_params=_params(("parallel", "arbitrary")),
        name="ffn",
    )(*args)


def _conv_body(h_ref, g_ref, win_ref, bin_ref, wdw_ref, bdw_ref, lng_ref, lnb_ref,
               wout_ref, bout_ref, out_ref, ubuf_ref, cbuf_ref, sh_ref, *, ts):
    n_lb = D_MODEL // LANES
    halo = CONV_HALO
    first_tap_row = halo - (CONV_WIDTH - 1)

    @pl.when(pl.program_id(1) == 0)
    def _():
        ubuf_ref[:, 0:halo, :] = jnp.zeros((n_lb, halo, LANES), F32)

    x = h_ref[...]
    xn = _rms(x, g_ref[...]).astype(BF16)
    y = jnp.dot(xn, win_ref[...], preferred_element_type=F32) + bin_ref[...]
    u = y[:, :D_MODEL] * _sigmoid(y[:, D_MODEL:])
    for j in range(n_lb):
        ubuf_ref[j, halo:halo + ts, :] = u[:, j * LANES:(j + 1) * LANES]

    def lane_block(j, carry):
        for r in range(PACK):
            n = ts + (PACK if r < PACK - 1 else 0)
            sh_ref[r, 0:n, :] = ubuf_ref[j, pl.ds(first_tap_row + r, n), :].astype(BF16)
        for g in range(ts // PACK):
            acc = None
            for k in range(CONV_WIDTH):
                a, r = divmod(k, PACK)
                seg = sh_ref[r, (g + a) * PACK:(g + a + 1) * PACK, :]
                term = seg.astype(F32) * wdw_ref[j, k].astype(F32)
                acc = term if acc is None else acc + term
            cbuf_ref[j, g * PACK:(g + 1) * PACK, :] = acc
        return carry

    lax.fori_loop(0, n_lb, lane_block, 0)

    ubuf_ref[:, 0:halo, :] = ubuf_ref[:, ts:ts + halo, :]

    c = jnp.concatenate([cbuf_ref[j] for j in range(n_lb)], axis=1) + bdw_ref[...]
    mu = jnp.mean(c, axis=-1, keepdims=True)
    d = c - mu
    var = jnp.mean(d * d, axis=-1, keepdims=True)
    z = d * lax.rsqrt(var + LN_EPS) * lng_ref[...] + lnb_ref[...]
    z = (z * _sigmoid(z)).astype(BF16)
    out_ref[...] = x + jnp.dot(z, wout_ref[...], preferred_element_type=F32) + bout_ref[...]


def _conv_layer(h, batch, seq, layer, norm_g, w_in, b_in, w_dw, b_dw, ln_g, ln_b, w_out, b_out):
    ts = CONV_TS
    n_s = seq // ts
    n_lb = D_MODEL // LANES
    wdw = jnp.broadcast_to(
        w_dw.astype(BF16).reshape(CONV_WIDTH, n_lb, 1, LANES).transpose(1, 0, 2, 3),
        (n_lb, CONV_WIDTH, PACK, LANES))
    row = lambda b, s: (b * n_s + s, 0)
    fixed = lambda b, s: (0, 0)
    vec = pl.BlockSpec((1, D_MODEL), fixed)
    return pl.pallas_call(
        functools.partial(_conv_body, ts=ts),
        grid=(batch, n_s),
        in_specs=[
            pl.BlockSpec((ts, D_MODEL), row),
            vec,
            pl.BlockSpec((None, D_MODEL, 2 * D_MODEL), lambda b, s: (layer, 0, 0)),
            pl.BlockSpec((1, 2 * D_MODEL), fixed),
            pl.BlockSpec((n_lb, CONV_WIDTH, PACK, LANES), lambda b, s: (0, 0, 0, 0)),
            vec, vec, vec,
            pl.BlockSpec((None, D_MODEL, D_MODEL), lambda b, s: (layer, 0, 0)),
            vec,
        ],
        out_specs=pl.BlockSpec((ts, D_MODEL), row),
        out_shape=jax.ShapeDtypeStruct(h.shape, F32),
        scratch_shapes=[
            pltpu.VMEM((n_lb, CONV_HALO + ts, LANES), F32),
            pltpu.VMEM((n_lb, ts, LANES), F32),
            pltpu.VMEM((PACK, ts + PACK, LANES), BF16),
        ],
        compiler_params=_params(("parallel", "arbitrary")),
        name="conformer_conv",
    )(h, norm_g.reshape(1, -1), w_in, b_in.reshape(1, -1), wdw, b_dw.reshape(1, -1),
      ln_g.reshape(1, -1), ln_b.reshape(1, -1), w_out, b_out.reshape(1, -1))


def _bucket_tiles(t):
    assert t >= MAX_DISTANCE
    qp = np.arange(t)[:, None]
    kp = np.arange(t)[None, :]
    max_exact = NUM_BUCKETS // 2

    def bucket(n):
        nf = np.maximum(n, 1).astype(np.float32)
        large = max_exact + (np.log(nf / max_exact) / math.log(MAX_DISTANCE / max_exact)
                             * (NUM_BUCKETS - max_exact)).astype(np.int32)
        return np.where(n < max_exact, n, np.minimum(large, NUM_BUCKETS - 1))

    diag = np.where(kp <= qp, bucket(np.maximum(qp - kp, 0)), -1)
    left = bucket(qp + t - kp)
    return np.stack([diag, left]).astype(np.int32)


def _bias_body(rb_ref, bucket_ref, out_ref):
    h = pl.program_id(0)
    far = rb_ref[NUM_BUCKETS - 1, h]
    bk = bucket_ref[...]
    acc = jnp.full(bk.shape, NEG_INF, F32)
    for b in range(NUM_BUCKETS):
        acc = jnp.where(bk == b, rb_ref[b, h] - far, acc)
    out_ref[...] = acc


def _rel_bias_tiles(rel_bias, t):
    return pl.pallas_call(
        _bias_body,
        grid=(N_HEADS,),
        in_specs=[
            pl.BlockSpec(memory_space=pltpu.SMEM),
            pl.BlockSpec((2, t, t), lambda h: (0, 0, 0)),
        ],
        out_specs=pl.BlockSpec((None, 2, t, t), lambda h: (h, 0, 0, 0)),
        out_shape=jax.ShapeDtypeStruct((N_HEADS, 2, t, t), F32),
        compiler_params=_params(("parallel",)),
        name="rel_bias_tiles",
    )(rel_bias.astype(F32), jnp.asarray(_bucket_tiles(t)))


def _attn_body(q_ref, kt_ref, v_ref, bias_ref, lam_ref, sub_ref, out_ref, vaug_ref,
               *, t, n_q, lambda_init):
    seq = v_ref.shape[0]
    lq = lam_ref[...]
    lam = (jnp.exp(jnp.sum(lq[0:1] * lq[1:2], axis=-1, keepdims=True))
           - jnp.exp(jnp.sum(lq[2:3] * lq[3:4], axis=-1, keepdims=True)) + lambda_init)
    for hh in range(ATT_HEADS):
        vaug_ref[hh, :, :LANES] = v_ref[:, hh * LANES:(hh + 1) * LANES]
        vaug_ref[hh, :, LANES:] = jnp.ones((seq, LANES), BF16)

    for hh, i in [(hh, i) for hh in range(ATT_HEADS) for i in reversed(range(n_q))]:
        head = slice(hh * LANES, (hh + 1) * LANES)
        q = q_ref[i * t:(i + 1) * t, head]
        lane = lax.broadcasted_iota(jnp.int32, q.shape, 1)
        zero = jnp.zeros_like(q)
        q12 = jnp.concatenate([jnp.where(lane < HEAD_DIM, q, zero),
                               jnp.where(lane >= HEAD_DIM, q, zero)], axis=0)
        pieces = []
        if i >= 2:
            pieces.append((0, (i - 1) * t, None))
        if i >= 1:
            pieces.append(((i - 1) * t, i * t, 1))
        pieces.append((i * t, (i + 1) * t, 0))

        scores = []
        colmax = None
        for c0, c1, bias_idx in pieces:
            s = jnp.dot(q12, kt_ref[head, c0:c1], preferred_element_type=F32)
            if bias_idx is not None:
                s = (s.reshape(2, t, t) + bias_ref[hh, bias_idx][None]).reshape(2 * t, t)
            for c in range((c1 - c0) // LANES):
                blk = s[:, c * LANES:(c + 1) * LANES]
                colmax = blk if colmax is None else jnp.maximum(colmax, blk)
            scores.append(s)
        m = jnp.max(colmax, axis=1, keepdims=True)

        acc = None
        for (c0, c1, _), s in zip(pieces, scores):
            p = jnp.exp(s - m).astype(BF16)
            d = jnp.dot(p, vaug_ref[hh, c0:c1, :], preferred_element_type=F32)
            acc = d if acc is None else acc + d
        o12 = acc[:, :LANES] / acc[:, LANES:]
        o = o12[:t] - lam * o12[t:]
        out_ref[i * t:(i + 1) * t, head] = (
            _rms(o, sub_ref[...]) * (1.0 - lambda_init)).astype(out_ref.dtype)


def _attention(q, kt, v, bias_tiles, lam_rows, subln, batch, seq, lambda_init):
    t = ATT_T
    n_q = seq // t
    q3 = q.reshape(batch, seq, QK_DIM)
    width = ATT_HEADS * LANES
    head_cols = pl.BlockSpec((None, seq, width), lambda b, h: (b, 0, h))
    out = pl.pallas_call(
        functools.partial(_attn_body, t=t, n_q=n_q, lambda_init=lambda_init),
        grid=(batch, N_HEADS // ATT_HEADS),
        in_specs=[
            head_cols,
            pl.BlockSpec((None, width, seq), lambda b, h: (b, h, 0)),
            head_cols,
            pl.BlockSpec((ATT_HEADS, 2, t, t), lambda b, h: (h, 0, 0, 0)),
            pl.BlockSpec((4, HEAD_DIM), lambda b, h: (0, 0)),
            pl.BlockSpec((1, LANES), lambda b, h: (0, 0)),
        ],
        out_specs=head_cols,
        out_shape=jax.ShapeDtypeStruct((batch, seq, V_DIM), BF16),
        scratch_shapes=[pltpu.VMEM((ATT_HEADS, seq, 2 * LANES), BF16)],
        compiler_params=_params(("parallel", "parallel")),
        name="diff_attention",
    )(q3, kt, v, bias_tiles, lam_rows, subln.reshape(1, LANES))
    return out.reshape(batch * seq, V_DIM)


def _kv_body(h_ref, g_ref, wkt_ref, wv_ref, kt_ref, v_ref):
    xn = _rms(h_ref[...], g_ref[...]).astype(BF16)
    kt = lax.dot_general(wkt_ref[...], xn, (((1,), (1,)), ((), ())),
                         preferred_element_type=F32)
    kt_ref[...] = kt.astype(BF16)
    v_ref[...] = jnp.dot(xn, wv_ref[...], preferred_element_type=F32).astype(BF16)


def _shared_kv(h, batch, seq, norm_g, w_kt, w_v):
    tm = PROJ_TM
    n_s = seq // tm
    fixed = lambda b, s: (0, 0)
    return pl.pallas_call(
        _kv_body,
        grid=(batch, n_s),
        in_specs=[
            pl.BlockSpec((tm, D_MODEL), lambda b, s: (b * n_s + s, 0)),
            pl.BlockSpec((1, D_MODEL), fixed),
            pl.BlockSpec((QK_DIM, D_MODEL), fixed),
            pl.BlockSpec((D_MODEL, V_DIM), fixed),
        ],
        out_specs=[
            pl.BlockSpec((None, QK_DIM, tm), lambda b, s: (b, 0, s)),
            pl.BlockSpec((None, tm, V_DIM), lambda b, s: (b, s, 0)),
        ],
        out_shape=[
            jax.ShapeDtypeStruct((batch, QK_DIM, seq), BF16),
            jax.ShapeDtypeStruct((batch, seq, V_DIM), BF16),
        ],
        compiler_params=_params(("parallel", "parallel")),
        name="shared_kv",
    )(h, norm_g.reshape(1, D_MODEL), w_kt, w_v)


def _cast_body(win_ref, wout_ref, cwin_ref, cwout_ref):
    cwin_ref[...] = win_ref[...].astype(BF16)
    cwout_ref[...] = wout_ref[...].astype(BF16)


def _cast_ffn_weights(w_in, w_out, layer):
    n = CAST_STEPS
    r_in, r_out = D_MODEL // n, D_FF // n
    slab = lambda s: (0, s, 0)
    return pl.pallas_call(
        _cast_body,
        grid=(n,),
        in_specs=[
            pl.BlockSpec((1, r_in, 2 * D_FF), lambda s: (layer, s, 0)),
            pl.BlockSpec((1, r_out, D_MODEL), lambda s: (layer, s, 0)),
        ],
        out_specs=[pl.BlockSpec((1, r_in, 2 * D_FF), slab), pl.BlockSpec((1, r_out, D_MODEL), slab)],
        out_shape=[jax.ShapeDtypeStruct((1, D_MODEL, 2 * D_FF), BF16),
                   jax.ShapeDtypeStruct((1, D_FF, D_MODEL), BF16)],
        compiler_params=_params(("parallel",)),
        name="cast_ffn_weights",
    )(w_in, w_out)


def _head_major(w):
    lead = w.shape[:-1]
    w = w.reshape(*lead, 2, N_HEADS, HEAD_DIM)
    return jnp.swapaxes(w, -3, -2).reshape(*lead, QK_DIM)


def kernel(x, p, ffn1_norm, ffn1_w_in, ffn1_w_out, mix_norm, ffn2_norm, ffn2_w_in, ffn2_w_out,
           ple_norm, ple_w_gate, ple_w_proj, conv_w_in, conv_b_in, conv_w_dw, conv_b_dw,
           conv_ln_g, conv_ln_b, conv_w_out, conv_b_out, kv_norm, w_kv, attn_w_q,
           attn_lq1, attn_lk1, attn_lq2, attn_lk2, attn_subln, attn_w_o, rel_bias, final_norm):
    batch, seq, d_model = x.shape
    tokens = batch * seq
    assert d_model == D_MODEL and p.shape == (DEPTH, batch, seq, PLE_DIM)
    assert tokens % FFN_TM == 0 and seq % CONV_TS == 0 and seq % PROJ_TM == 0 and seq % ATT_T == 0
    h = x.reshape(tokens, D_MODEL)
    p2 = p.reshape(DEPTH, tokens, PLE_DIM)
    bias_tiles = _rel_bias_tiles(rel_bias, ATT_T)
    w_kt = _head_major(w_kv[:, :QK_DIM]).T.astype(BF16)
    w_v = w_kv[:, QK_DIM:].astype(BF16)
    ple_w_gate, ple_w_proj = ple_w_gate.astype(BF16), ple_w_proj.astype(BF16)
    conv_w_in, conv_w_out = conv_w_in.astype(BF16), conv_w_out.astype(BF16)
    attn_w_o = attn_w_o.astype(BF16)
    w_q = (_head_major(attn_w_q) * (HEAD_DIM ** -0.5)).astype(BF16)
    w_in, w_out = _cast_ffn_weights(ffn1_w_in, ffn1_w_out, 0)
    kt = v = o = None
    for i in range(DEPTH):
        j = i - N_A
        if i == N_A:
            kt, v = _shared_kv(h, batch, seq, kv_norm, w_kt, w_v)
        res = _ffn(h, 0, ffn1_norm[i], w_in, w_out,
                   q_out=(mix_norm[i], j, w_q) if i >= N_A else None,
                   cast_next=(ffn2_w_in, ffn2_w_out, i))
        h, w_in, w_out = res[0], res[-2][None], res[-1][None]
        if i < N_A:
            h = _conv_layer(h, batch, seq, i, mix_norm[i], conv_w_in, conv_b_in[i],
                            conv_w_dw[i], conv_b_dw[i], conv_ln_g[i], conv_ln_b[i],
                            conv_w_out, conv_b_out[i])
        else:
            lambda_init = 0.8 - 0.6 * math.exp(-0.3 * i)
            lam_rows = jnp.stack([attn_lq1[j], attn_lk1[j], attn_lq2[j], attn_lk2[j]]).astype(F32)
            o = _attention(res[1], kt, v, bias_tiles, lam_rows, attn_subln[j], batch, seq,
                           lambda_init)
        last = i == DEPTH - 1
        res = _ffn(h, 0, ffn2_norm[i], w_in, w_out,
                   attn_in=(o, j, attn_w_o) if i >= N_A else None,
                   ple=(ple_norm[i], i, ple_w_gate, p2, ple_w_proj),
                   final_g=final_norm if last else None,
                   cast_next=None if last else (ffn1_w_in, ffn1_w_out, i + 1))
        h = res[0]
        if not last:
            w_in, w_out = res[-2][None], res[-1][None]
    return h.reshape(batch, seq, D_MODEL)
```

```python
import functools
import math

import jax
import jax.numpy as jnp
import numpy as np
from jax import lax
from jax.experimental import pallas as pl
from jax.experimental.pallas import tpu as pltpu

D_MODEL = 1024
DEPTH = 4
N_A = DEPTH // 2
D_FF = 4 * D_MODEL
CONV_WIDTH = 31
HEAD_DIM = 64
N_HEADS = D_MODEL // (2 * HEAD_DIM)
QK_DIM = 2 * N_HEADS * HEAD_DIM
V_DIM = N_HEADS * 2 * HEAD_DIM
NUM_BUCKETS = 32
MAX_DISTANCE = 128
PLE_DIM = 256
RMS_EPS = 1e-6
LN_EPS = 1e-5
NEG_INF = -1e30

LANES = 128
PACK = 16
VMEM_LIMIT_BYTES = 56 * 1024 * 1024

FFN_TM = 1024
FFN_TF = 1024
FFN_TF_BARE = 2048
FFN_SUB = 256
PROJ_TM = 2048
CAST_STEPS = 8
CONV_TS = 1024
CONV_HALO = 32
ATT_T = 256
ATT_HEADS = 2

F32 = jnp.float32
BF16 = jnp.bfloat16


def _rms(x, g):
    ms = jnp.mean(x * x, axis=-1, keepdims=True)
    return x * lax.rsqrt(ms + RMS_EPS) * g


def _sigmoid(x):
    return 1.0 / (1.0 + jnp.exp(-x))


def _params(sem):
    return pltpu.CompilerParams(dimension_semantics=sem, vmem_limit_bytes=VMEM_LIMIT_BYTES)


def _ffn_body(*refs, n_f, attn_in, ple, final, q_out, cast):
    it = iter(refs)
    h_ref, g_ref, wg_ref, wu_ref, wo_ref = (next(it) for _ in range(5))
    if attn_in:
        o_ref, wproj_ref = next(it), next(it)
    if ple:
        pn_ref, pwg_ref, p_ref, pwp_ref = (next(it) for _ in range(4))
    if final:
        fn_ref = next(it)
    if q_out:
        qg_ref, wq_ref = next(it), next(it)
    if cast:
        nwin_ref, nwout_ref = next(it), next(it)
    out_ref = next(it)
    if q_out:
        q_ref = next(it)
    if cast:
        cwin_ref, cwout_ref = next(it), next(it)
    xn_ref = next(it)

    f = pl.program_id(1)

    if cast:
        cwin_ref[...] = nwin_ref[...].astype(BF16)
        cwout_ref[...] = nwout_ref[...].astype(BF16)

    def step(first):
        if first:
            x = h_ref[...]
            if attn_in:
                x = x + jnp.dot(o_ref[...], wproj_ref[...], preferred_element_type=F32)
                out_ref[...] = x
            xn = _rms(x, g_ref[...]).astype(BF16)
            xn_ref[...] = xn
        else:
            xn = xn_ref[...]
        part = None
        for c in range(wg_ref.shape[1] // FFN_SUB):
            cols = slice(c * FFN_SUB, (c + 1) * FFN_SUB)
            gate = jnp.dot(xn, wg_ref[:, cols], preferred_element_type=F32)
            up = jnp.dot(xn, wu_ref[:, cols], preferred_element_type=F32)
            act = (gate * _sigmoid(gate) * up).astype(BF16)
            d = jnp.dot(act, wo_ref[cols, :], preferred_element_type=F32)
            part = d if part is None else part + d
        if first and not attn_in:
            out_ref[...] = h_ref[...] + 0.5 * part
        else:
            out_ref[...] += 0.5 * part

    pl.when(f == 0)(functools.partial(step, True))
    pl.when(f > 0)(functools.partial(step, False))

    @pl.when(f == n_f - 1)
    def _():
        hn = out_ref[...]
        if ple:
            xg = _rms(hn, pn_ref[...]).astype(BF16)
            gt = _sigmoid(jnp.dot(xg, pwg_ref[...], preferred_element_type=F32))
            pr = jnp.dot(p_ref[...].astype(BF16), pwp_ref[...], preferred_element_type=F32)
            hn = hn + gt * pr
        if final:
            hn = _rms(hn, fn_ref[...])
        out_ref[...] = hn
        if q_out:
            xq = _rms(hn, qg_ref[...]).astype(BF16)
            q_ref[...] = jnp.dot(xq, wq_ref[...], preferred_element_type=F32).astype(BF16)


def _ffn(h, layer, norm_g, w_in, w_out, attn_in=None, ple=None, final_g=None, q_out=None,
         cast_next=None):
    t = h.shape[0]
    tm = FFN_TM
    bare = attn_in is None and ple is None and q_out is None
    tf = FFN_TF_BARE if bare else FFN_TF
    n_t, n_f = t // tm, D_FF // tf
    row = lambda i, f: (i, 0)
    fixed = lambda i, f: (0, 0)
    once = pl.Buffered(1)
    vec = pl.BlockSpec((1, D_MODEL), fixed, pipeline_mode=once)

    def square(lyr):
        return pl.BlockSpec((None, D_MODEL, D_MODEL), lambda i, f: (lyr, 0, 0), pipeline_mode=once)

    in_specs = [
        pl.BlockSpec((tm, D_MODEL), row),
        vec,
        pl.BlockSpec((None, D_MODEL, tf), lambda i, f: (layer, 0, f)),
        pl.BlockSpec((None, D_MODEL, tf), lambda i, f: (layer, 0, f + n_f)),
        pl.BlockSpec((None, tf, D_MODEL), lambda i, f: (layer, f, 0)),
    ]
    args = [h, norm_g.reshape(1, D_MODEL), w_in, w_in, w_out]
    if attn_in is not None:
        o, proj_layer, w_proj = attn_in
        in_specs += [pl.BlockSpec((tm, V_DIM), row), square(proj_layer)]
        args += [o, w_proj]
    if ple is not None:
        pn, ple_layer, pwg, p, pwp = ple
        in_specs += [
            vec,
            square(ple_layer),
            pl.BlockSpec((None, tm, PLE_DIM), lambda i, f: (ple_layer, i, 0)),
            pl.BlockSpec((None, PLE_DIM, D_MODEL), lambda i, f: (ple_layer, 0, 0),
                         pipeline_mode=once),
        ]
        args += [pn.reshape(1, D_MODEL), pwg, p, pwp]
    if final_g is not None:
        in_specs.append(vec)
        args.append(final_g.reshape(1, D_MODEL))
    if q_out is not None:
        g_q, q_layer, w_q = q_out
        in_specs += [vec, square(q_layer)]
        args += [g_q.reshape(1, D_MODEL), w_q]
    out_specs = [pl.BlockSpec((tm, D_MODEL), row)]
    out_shape = [jax.ShapeDtypeStruct((t, D_MODEL), F32)]
    if q_out is not None:
        out_specs.append(pl.BlockSpec((tm, QK_DIM), row))
        out_shape.append(jax.ShapeDtypeStruct((t, QK_DIM), BF16))
    if cast_next is not None:
        nw_in, nw_out, n_layer = cast_next
        steps = n_t * n_f
        r_in, r_out = D_MODEL // steps, D_FF // steps
        slab = lambda i, f: (i * n_f + f, 0)
        in_specs += [
            pl.BlockSpec((None, r_in, 2 * D_FF), lambda i, f: (n_layer, i * n_f + f, 0)),
            pl.BlockSpec((None, r_out, D_MODEL), lambda i, f: (n_layer, i * n_f + f, 0)),
        ]
        args += [nw_in, nw_out]
        out_specs += [pl.BlockSpec((r_in, 2 * D_FF), slab), pl.BlockSpec((r_out, D_MODEL), slab)]
        out_shape += [jax.ShapeDtypeStruct((D_MODEL, 2 * D_FF), BF16),
                      jax.ShapeDtypeStruct((D_FF, D_MODEL), BF16)]
    body = functools.partial(_ffn_body, n_f=n_f, attn_in=attn_in is not None, ple=ple is not None,
                             final=final_g is not None, q_out=q_out is not None,
                             cast=cast_next is not None)
    return pl.pallas_call(
        body,
        grid=(n_t, n_f),
        in_specs=in_specs,
        out_specs=out_specs,
        out_shape=out_shape,
        scratch_shapes=[pltpu.VMEM((tm, D_MODEL), BF16)],
        compiler_params=_params(("parallel", "arbitrary")),
        name="ffn",
    )(*args)


def _conv_body(h_ref, g_ref, win_ref, bin_ref, wdw_ref, bdw_ref, lng_ref, lnb_ref,
               wout_ref, bout_ref, out_ref, ubuf_ref, cbuf_ref, sh_ref, *, ts):
    n_lb = D_MODEL // LANES
    halo = CONV_HALO
    first_tap_row = halo - (CONV_WIDTH - 1)

    @pl.when(pl.program_id(1) == 0)
    def _():
        ubuf_ref[:, 0:halo, :] = jnp.zeros((n_lb, halo, LANES), F32)

    x = h_ref[...]
    xn = _rms(x, g_ref[...]).astype(BF16)
    y = jnp.dot(xn, win_ref[...], preferred_element_type=F32) + bin_ref[...]
    u = y[:, :D_MODEL] * _sigmoid(y[:, D_MODEL:])
    for j in range(n_lb):
        ubuf_ref[j, halo:halo + ts, :] = u[:, j * LANES:(j + 1) * LANES]

    def lane_block(j, carry):
        for r in range(PACK):
            n = ts + (PACK if r < PACK - 1 else 0)
            sh_ref[r, 0:n, :] = ubuf_ref[j, pl.ds(first_tap_row + r, n), :].astype(BF16)
        for g in range(ts // PACK):
            acc = None
            for k in range(CONV_WIDTH):
                a, r = divmod(k, PACK)
                seg = sh_ref[r, (g + a) * PACK:(g + a + 1) * PACK, :]
                term = seg.astype(F32) * wdw_ref[j, k].astype(F32)
                acc = term if acc is None else acc + term
            cbuf_ref[j, g * PACK:(g + 1) * PACK, :] = acc
        return carry

    lax.fori_loop(0, n_lb, lane_block, 0)

    ubuf_ref[:, 0:halo, :] = ubuf_ref[:, ts:ts + halo, :]

    c = jnp.concatenate([cbuf_ref[j] for j in range(n_lb)], axis=1) + bdw_ref[...]
    mu = jnp.mean(c, axis=-1, keepdims=True)
    d = c - mu
    var = jnp.mean(d * d, axis=-1, keepdims=True)
    z = d * lax.rsqrt(var + LN_EPS) * lng_ref[...] + lnb_ref[...]
    z = (z * _sigmoid(z)).astype(BF16)
    out_ref[...] = x + jnp.dot(z, wout_ref[...], preferred_element_type=F32) + bout_ref[...]


def _conv_layer(h, batch, seq, layer, norm_g, w_in, b_in, w_dw, b_dw, ln_g, ln_b, w_out, b_out):
    ts = CONV_TS
    n_s = seq // ts
    n_lb = D_MODEL // LANES
    wdw = jnp.broadcast_to(
        w_dw.astype(BF16).reshape(CONV_WIDTH, n_lb, 1, LANES).transpose(1, 0, 2, 3),
        (n_lb, CONV_WIDTH, PACK, LANES))
    row = lambda b, s: (b * n_s + s, 0)
    fixed = lambda b, s: (0, 0)
    vec = pl.BlockSpec((1, D_MODEL), fixed)
    return pl.pallas_call(
        functools.partial(_conv_body, ts=ts),
        grid=(batch, n_s),
        in_specs=[
            pl.BlockSpec((ts, D_MODEL), row),
            vec,
            pl.BlockSpec((None, D_MODEL, 2 * D_MODEL), lambda b, s: (layer, 0, 0)),
            pl.BlockSpec((1, 2 * D_MODEL), fixed),
            pl.BlockSpec((n_lb, CONV_WIDTH, PACK, LANES), lambda b, s: (0, 0, 0, 0)),
            vec, vec, vec,
            pl.BlockSpec((None, D_MODEL, D_MODEL), lambda b, s: (layer, 0, 0)),
            vec,
        ],
        out_specs=pl.BlockSpec((ts, D_MODEL), row),
        out_shape=jax.ShapeDtypeStruct(h.shape, F32),
        scratch_shapes=[
            pltpu.VMEM((n_lb, CONV_HALO + ts, LANES), F32),
            pltpu.VMEM((n_lb, ts, LANES), F32),
            pltpu.VMEM((PACK, ts + PACK, LANES), BF16),
        ],
        compiler_params=_params(("parallel", "arbitrary")),
        name="conformer_conv",
    )(h, norm_g.reshape(1, -1), w_in, b_in.reshape(1, -1), wdw, b_dw.reshape(1, -1),
      ln_g.reshape(1, -1), ln_b.reshape(1, -1), w_out, b_out.reshape(1, -1))


def _bucket_tiles(t):
    assert t >= MAX_DISTANCE
    qp = np.arange(t)[:, None]
    kp = np.arange(t)[None, :]
    max_exact = NUM_BUCKETS // 2

    def bucket(n):
        nf = np.maximum(n, 1).astype(np.float32)
        large = max_exact + (np.log(nf / max_exact) / math.log(MAX_DISTANCE / max_exact)
                             * (NUM_BUCKETS - max_exact)).astype(np.int32)
        return np.where(n < max_exact, n, np.minimum(large, NUM_BUCKETS - 1))

    diag = np.where(kp <= qp, bucket(np.maximum(qp - kp, 0)), -1)
    left = bucket(qp + t - kp)
    return np.stack([diag, left]).astype(np.int32)


def _bias_body(rb_ref, bucket_ref, out_ref):
    h = pl.program_id(0)
    far = rb_ref[NUM_BUCKETS - 1, h]
    bk = bucket_ref[...]
    acc = jnp.full(bk.shape, NEG_INF, F32)
    for b in range(NUM_BUCKETS):
        acc = jnp.where(bk == b, rb_ref[b, h] - far, acc)
    out_ref[...] = acc


def _rel_bias_tiles(rel_bias, t):
    return pl.pallas_call(
        _bias_body,
        grid=(N_HEADS,),
        in_specs=[
            pl.BlockSpec(memory_space=pltpu.SMEM),
            pl.BlockSpec((2, t, t), lambda h: (0, 0, 0)),
        ],
        out_specs=pl.BlockSpec((None, 2, t, t), lambda h: (h, 0, 0, 0)),
        out_shape=jax.ShapeDtypeStruct((N_HEADS, 2, t, t), F32),
        compiler_params=_params(("parallel",)),
        name="rel_bias_tiles",
    )(rel_bias.astype(F32), jnp.asarray(_bucket_tiles(t)))


def _attn_body(q_ref, kt_ref, v_ref, bias_ref, lam_ref, sub_ref, out_ref, vaug_ref,
               *, t, n_q, lambda_init):
    seq = v_ref.shape[0]
    lq = lam_ref[...]
    lam = (jnp.exp(jnp.sum(lq[0:1] * lq[1:2], axis=-1, keepdims=True))
           - jnp.exp(jnp.sum(lq[2:3] * lq[3:4], axis=-1, keepdims=True)) + lambda_init)
    for hh in range(ATT_HEADS):
        vaug_ref[hh, :, :LANES] = v_ref[:, hh * LANES:(hh + 1) * LANES]
        vaug_ref[hh, :, LANES:] = jnp.ones((seq, LANES), BF16)

    for hh, i in [(hh, i) for hh in range(ATT_HEADS) for i in reversed(range(n_q))]:
        head = slice(hh * LANES, (hh + 1) * LANES)
        q = q_ref[i * t:(i + 1) * t, head]
        lane = lax.broadcasted_iota(jnp.int32, q.shape, 1)
        zero = jnp.zeros_like(q)
        q12 = jnp.concatenate([jnp.where(lane < HEAD_DIM, q, zero),
                               jnp.where(lane >= HEAD_DIM, q, zero)], axis=0)
        pieces = []
        if i >= 2:
            pieces.append((0, (i - 1) * t, None))
        if i >= 1:
            pieces.append(((i - 1) * t, i * t, 1))
        pieces.append((i * t, (i + 1) * t, 0))

        scores = []
        colmax = None
        for c0, c1, bias_idx in pieces:
            s = jnp.dot(q12, kt_ref[head, c0:c1], preferred_element_type=F32)
            if bias_idx is not None:
                s = (s.reshape(2, t, t) + bias_ref[hh, bias_idx][None]).reshape(2 * t, t)
            for c in range((c1 - c0) // LANES):
                blk = s[:, c * LANES:(c + 1) * LANES]
                colmax = blk if colmax is None else jnp.maximum(colmax, blk)
            scores.append(s)
        m = jnp.max(colmax, axis=1, keepdims=True)

        acc = None
        for (c0, c1, _), s in zip(pieces, scores):
            p = jnp.exp(s - m).astype(BF16)
            d = jnp.dot(p, vaug_ref[hh, c0:c1, :], preferred_element_type=F32)
            acc = d if acc is None else acc + d
        o12 = acc[:, :LANES] / acc[:, LANES:]
        o = o12[:t] - lam * o12[t:]
        out_ref[i * t:(i + 1) * t, head] = (
            _rms(o, sub_ref[...]) * (1.0 - lambda_init)).astype(out_ref.dtype)


def _attention(q, kt, v, bias_tiles, lam_rows, subln, batch, seq, lambda_init):
    t = ATT_T
    n_q = seq // t
    q3 = q.reshape(batch, seq, QK_DIM)
    width = ATT_HEADS * LANES
    ahead = pl.Buffered(3)
    head_cols = lambda mode: pl.BlockSpec((1, seq, width), lambda b, h: (b, 0, h),
                                          pipeline_mode=mode)

    def outer(q_hbm, kt_hbm, v_hbm, bias_hbm, lam_ref, sub_ref, out_hbm, vaug_ref):
        def step(q_ref, kt_ref, v_ref, bias_ref, out_ref):
            _attn_body(q_ref.at[0], kt_ref.at[0], v_ref.at[0], bias_ref, lam_ref, sub_ref,
                       out_ref.at[0], vaug_ref, t=t, n_q=n_q, lambda_init=lambda_init)

        pltpu.emit_pipeline(
            step,
            grid=(batch, N_HEADS // ATT_HEADS),
            in_specs=[
                head_cols(ahead),
                pl.BlockSpec((1, width, seq), lambda b, h: (b, h, 0), pipeline_mode=ahead),
                head_cols(ahead),
                pl.BlockSpec((ATT_HEADS, 2, t, t), lambda b, h: (h, 0, 0, 0), pipeline_mode=ahead),
            ],
            out_specs=[head_cols(None)],
        )(q_hbm, kt_hbm, v_hbm, bias_hbm, out_hbm)

    hbm = pl.BlockSpec(memory_space=pl.ANY)
    vmem = pl.BlockSpec(memory_space=pltpu.VMEM)
    out = pl.pallas_call(
        outer,
        in_specs=[hbm, hbm, hbm, hbm, vmem, vmem],
        out_specs=hbm,
        out_shape=jax.ShapeDtypeStruct((batch, seq, V_DIM), BF16),
        scratch_shapes=[pltpu.VMEM((ATT_HEADS, seq, 2 * LANES), BF16)],
        compiler_params=pltpu.CompilerParams(vmem_limit_bytes=VMEM_LIMIT_BYTES),
        name="diff_attention",
    )(q3, kt, v, bias_tiles, lam_rows, subln.reshape(1, LANES))
    return out.reshape(batch * seq, V_DIM)


def _kv_body(h_ref, g_ref, wkt_ref, wv_ref, kt_ref, v_ref):
    xn = _rms(h_ref[...], g_ref[...]).astype(BF16)
    kt = lax.dot_general(wkt_ref[...], xn, (((1,), (1,)), ((), ())),
                         preferred_element_type=F32)
    kt_ref[...] = kt.astype(BF16)
    v_ref[...] = jnp.dot(xn, wv_ref[...], preferred_element_type=F32).astype(BF16)


def _shared_kv(h, batch, seq, norm_g, w_kt, w_v):
    tm = PROJ_TM
    n_s = seq // tm
    fixed = lambda b, s: (0, 0)
    return pl.pallas_call(
        _kv_body,
        grid=(batch, n_s),
        in_specs=[
            pl.BlockSpec((tm, D_MODEL), lambda b, s: (b * n_s + s, 0)),
            pl.BlockSpec((1, D_MODEL), fixed),
            pl.BlockSpec((QK_DIM, D_MODEL), fixed),
            pl.BlockSpec((D_MODEL, V_DIM), fixed),
        ],
        out_specs=[
            pl.BlockSpec((None, QK_DIM, tm), lambda b, s: (b, 0, s)),
            pl.BlockSpec((None, tm, V_DIM), lambda b, s: (b, s, 0)),
        ],
        out_shape=[
            jax.ShapeDtypeStruct((batch, QK_DIM, seq), BF16),
            jax.ShapeDtypeStruct((batch, seq, V_DIM), BF16),
        ],
        compiler_params=_params(("parallel", "parallel")),
        name="shared_kv",
    )(h, norm_g.reshape(1, D_MODEL), w_kt, w_v)


def _cast_body(win_ref, wout_ref, cwin_ref, cwout_ref):
    cwin_ref[...] = win_ref[...].astype(BF16)
    cwout_ref[...] = wout_ref[...].astype(BF16)


def _cast_ffn_weights(w_in, w_out, layer):
    n = CAST_STEPS
    r_in, r_out = D_MODEL // n, D_FF // n
    slab = lambda s: (0, s, 0)
    return pl.pallas_call(
        _cast_body,
        grid=(n,),
        in_specs=[
            pl.BlockSpec((1, r_in, 2 * D_FF), lambda s: (layer, s, 0)),
            pl.BlockSpec((1, r_out, D_MODEL), lambda s: (layer, s, 0)),
        ],
        out_specs=[pl.BlockSpec((1, r_in, 2 * D_FF), slab), pl.BlockSpec((1, r_out, D_MODEL), slab)],
        out_shape=[jax.ShapeDtypeStruct((1, D_MODEL, 2 * D_FF), BF16),
                   jax.ShapeDtypeStruct((1, D_FF, D_MODEL), BF16)],
        compiler_params=_params(("parallel",)),
        name="cast_ffn_weights",
    )(w_in, w_out)


def _head_major(w):
    lead = w.shape[:-1]
    w = w.reshape(*lead, 2, N_HEADS, HEAD_DIM)
    return jnp.swapaxes(w, -3, -2).reshape(*lead, QK_DIM)


def kernel(x, p, ffn1_norm, ffn1_w_in, ffn1_w_out, mix_norm, ffn2_norm, ffn2_w_in, ffn2_w_out,
           ple_norm, ple_w_gate, ple_w_proj, conv_w_in, conv_b_in, conv_w_dw, conv_b_dw,
           conv_ln_g, conv_ln_b, conv_w_out, conv_b_out, kv_norm, w_kv, attn_w_q,
           attn_lq1, attn_lk1, attn_lq2, attn_lk2, attn_subln, attn_w_o, rel_bias, final_norm):
    batch, seq, d_model = x.shape
    tokens = batch * seq
    assert d_model == D_MODEL and p.shape == (DEPTH, batch, seq, PLE_DIM)
    assert tokens % FFN_TM == 0 and seq % CONV_TS == 0 and seq % PROJ_TM == 0 and seq % ATT_T == 0
    h = x.reshape(tokens, D_MODEL)
    p2 = p.reshape(DEPTH, tokens, PLE_DIM)
    bias_tiles = _rel_bias_tiles(rel_bias, ATT_T)
    w_kt = _head_major(w_kv[:, :QK_DIM]).T.astype(BF16)
    w_v = w_kv[:, QK_DIM:].astype(BF16)
    ple_w_gate, ple_w_proj = ple_w_gate.astype(BF16), ple_w_proj.astype(BF16)
    conv_w_in, conv_w_out = conv_w_in.astype(BF16), conv_w_out.astype(BF16)
    attn_w_o = attn_w_o.astype(BF16)
    w_q = (_head_major(attn_w_q) * (HEAD_DIM ** -0.5)).astype(BF16)
    w_in, w_out = _cast_ffn_weights(ffn1_w_in, ffn1_w_out, 0)
    kt = v = o = None
    for i in range(DEPTH):
        j = i - N_A
        if i == N_A:
            kt, v = _shared_kv(h, batch, seq, kv_norm, w_kt, w_v)
        res = _ffn(h, 0, ffn1_norm[i], w_in, w_out,
                   q_out=(mix_norm[i], j, w_q) if i >= N_A else None,
                   cast_next=(ffn2_w_in, ffn2_w_out, i))
        h, w_in, w_out = res[0], res[-2][None], res[-1][None]
        if i < N_A:
            h = _conv_layer(h, batch, seq, i, mix_norm[i], conv_w_in, conv_b_in[i],
                            conv_w_dw[i], conv_b_dw[i], conv_ln_g[i], conv_ln_b[i],
                            conv_w_out, conv_b_out[i])
        else:
            lambda_init = 0.8 - 0.6 * math.exp(-0.3 * i)
            lam_rows = jnp.stack([attn_lq1[j], attn_lk1[j], attn_lq2[j], attn_lk2[j]]).astype(F32)
            o = _attention(res[1], kt, v, bias_tiles, lam_rows, attn_subln[j], batch, seq,
                           lambda_init)
        last = i == DEPTH - 1
        res = _ffn(h, 0, ffn2_norm[i], w_in, w_out,
                   attn_in=(o, j, attn_w_o) if i >= N_A else None,
                   ple=(ple_norm[i], i, ple_w_gate, p2, ple_w_proj),
                   final_g=final_norm if last else None,
                   cast_next=None if last else (ffn1_w_in, ffn1_w_out, i + 1))
        h = res[0]
        if not last:
            w_in, w_out = res[-2][None], res[-1][None]
    return h.reshape(batch, seq, D_MODEL)
```
